```python
import math
import jax, jax.numpy as jnp
from jax import lax
import numpy as np

D_MODEL = 1024
BATCH = 32
SEQ = 2048
DEPTH = 1
DEC_BATCH = 128
DEC_SEQ = 1
PAST_LEN = 8192
PAGE_SIZE = 128

HEAD_DIM = 64
N_DIFF_HEADS = 8
N_DIFF_KV = 4
GQA_REP = N_DIFF_HEADS // N_DIFF_KV
Q_BLOCK = 128
Q_WIDTH = N_DIFF_HEADS * 2 * HEAD_DIM
K_WIDTH = N_DIFF_KV * 2 * HEAD_DIM
V_WIDTH = N_DIFF_KV * 2 * HEAD_DIM
ATTN_OUT_WIDTH = N_DIFF_HEADS * 2 * HEAD_DIM
D_INNER = 2 * D_MODEL
SSM_HEAD_DIM = 64
N_SSM_HEADS = D_INNER // SSM_HEAD_DIM
N_SSM_GROUPS = 4
HEADS_PER_GROUP = N_SSM_HEADS // N_SSM_GROUPS
D_STATE = 128
CONV_WIDTH = 4
CONV_DIM = D_INNER + 2 * N_SSM_GROUPS * D_STATE
SSD_CHUNK = 128
N_MEM = 256
N_MEM_HEADS = 4
MEM_HEAD_DIM = D_MODEL // N_MEM_HEADS
D_FF = 4 * D_MODEL
EPS = 1e-6
IN_SIZES = (Q_WIDTH, K_WIDTH, V_WIDTH, D_INNER, CONV_DIM, N_SSM_HEADS, D_MODEL, D_MODEL)
IN_WIDTH = Q_WIDTH + K_WIDTH + V_WIDTH + D_INNER + CONV_DIM + N_SSM_HEADS + 2 * D_MODEL

kernel_name = 'hybrid_diffattn_ssd_decoder_step'


def rms_norm(x, g):
    xf = x.astype(jnp.float32)
    y = xf * lax.rsqrt(jnp.mean(xf * xf, axis=-1, keepdims=True) + EPS)
    return (y * g.astype(jnp.float32)).astype(x.dtype)


def split_cols(x, sizes):
    out, off = [], 0
    for s in sizes:
        out.append(x[..., off:off + s])
        off += s
    return out


def alibi_slopes():
    s = 2.0 ** (-8.0 * np.arange(1, N_DIFF_HEADS + 1) / N_DIFF_HEADS)
    return jnp.asarray(s.astype(np.float32)).reshape(N_DIFF_KV, GQA_REP)


def diff_attn_block(q, k, v, q_pos, k_pos, lam, sub_g, lambda_init):
    s = jnp.einsum('bqkrid,bskid->bkriqs', q, k) * (HEAD_DIM ** -0.5)
    dist = (q_pos[:, None] - k_pos[None, :]).astype(jnp.float32)
    bias = -alibi_slopes()[:, :, None, None, None] * dist
    s = jnp.where(dist >= 0, s + bias, -jnp.inf)
    a = jax.nn.softmax(s, axis=-1)
    w = a[:, :, :, 0] - lam * a[:, :, :, 1]
    o = jnp.einsum('bkrqs,bskv->bqkrv', w, v)
    o = rms_norm(o, sub_g) * (1.0 - lambda_init)
    return o.reshape(o.shape[0], o.shape[1], ATTN_OUT_WIDTH)


def diff_attention(q, k, v, q_pos, k_pos, lam, sub_g, lambda_init):
    b, t = q.shape[0], q.shape[1]
    q = q.astype(jnp.float32)
    k = k.astype(jnp.float32)
    v = v.astype(jnp.float32)
    if t > Q_BLOCK and t % Q_BLOCK == 0:
        nb = t // Q_BLOCK
        qb = jnp.moveaxis(q.reshape(b, nb, Q_BLOCK, *q.shape[2:]), 1, 0)
        pb = q_pos.reshape(nb, Q_BLOCK)
        ob = lax.map(lambda a: diff_attn_block(a[0], k, v, a[1], k_pos, lam, sub_g, lambda_init), (qb, pb))
        return jnp.moveaxis(ob, 0, 1).reshape(b, t, ATTN_OUT_WIDTH)
    return diff_attn_block(q, k, v, q_pos, k_pos, lam, sub_g, lambda_init)


def segsum_decay(a_cs):
    n = a_cs.shape[-1]
    diff = a_cs[..., :, None] - a_cs[..., None, :]
    mask = jnp.tril(jnp.ones((n, n), dtype=bool))
    return jnp.exp(jnp.where(mask, diff, -jnp.inf))


def ssd(x, dt, a, b_in, c_in, h0):
    bsz, L = x.shape[0], x.shape[1]
    cl = math.gcd(L, SSD_CHUNK)
    nc = L // cl
    G, J, P, N = N_SSM_GROUPS, HEADS_PER_GROUP, SSM_HEAD_DIM, D_STATE
    x = x.reshape(bsz, nc, cl, G, J, P)
    dt = dt.reshape(bsz, nc, cl, G, J)
    bm = b_in.reshape(bsz, nc, cl, G, N)
    cm = c_in.reshape(bsz, nc, cl, G, N)
    a_cs = jnp.cumsum(dt * a, axis=2)
    decay = segsum_decay(jnp.moveaxis(a_cs, 2, -1))
    cb = jnp.einsum('bclgn,bcsgn->bcgls', cm, bm)
    y_diag = jnp.einsum('bcgls,bcgjls,bcsgj,bcsgjp->bclgjp', cb, decay, dt, x)
    decay_end = jnp.exp(a_cs[:, :, -1:] - a_cs)
    chunk_states = jnp.einsum('bcsgn,bcsgj,bcsgjp->bcgjpn', bm, decay_end * dt, x)
    chunk_decay = jnp.exp(a_cs[:, :, -1])

    def step(h, inp):
        st, dec = inp
        return h * dec[..., None, None] + st, h

    h_last, h_prev = lax.scan(step, h0, (jnp.moveaxis(chunk_states, 1, 0), jnp.moveaxis(chunk_decay, 1, 0)))
    h_prev = jnp.moveaxis(h_prev, 0, 1)
    y_off = jnp.einsum('bclgn,bcgjpn,bclgj->bclgjp', cm, h_prev, jnp.exp(a_cs))
    return (y_diag + y_off).reshape(bsz, L, G, J, P), h_last


def ssd_branch(z, xbc, dt_raw, ssm0, conv0, lp):
    b, t = xbc.shape[0], xbc.shape[1]
    G, J, P, N = N_SSM_GROUPS, HEADS_PER_GROUP, SSM_HEAD_DIM, D_STATE
    xpad = jnp.concatenate([conv0.astype(xbc.dtype), xbc], axis=1)
    conv = lp['conv_b']
    for j in range(CONV_WIDTH):
        conv = conv + xpad[:, j:j + t] * lp['conv_w'][j]
    new_conv = xpad[:, -(CONV_WIDTH - 1):]
    xbc_act = jax.nn.silu(conv.astype(jnp.float32))
    xs, bs, cs = split_cols(xbc_act, (D_INNER, G * N, G * N))
    dt = jax.nn.softplus(dt_raw.astype(jnp.float32) + lp['dt_bias'].astype(jnp.float32))
    a = -jnp.exp(lp['a_log'].astype(jnp.float32)).reshape(G, J)
    xs = xs.reshape(b, t, G, J, P)
    y, h_last = ssd(xs, dt.reshape(b, t, G, J), a, bs.reshape(b, t, G, N), cs.reshape(b, t, G, N),
                    ssm0.astype(jnp.float32).reshape(b, G, J, P, N))
    y = y + lp['d_skip'].astype(jnp.float32).reshape(G, J, 1) * xs
    y = y.reshape(b, t, D_INNER) * jax.nn.silu(z.astype(jnp.float32))
    y = rms_norm(y.reshape(b, t, G, D_INNER // G), lp['ssm_norm_g'].reshape(G, D_INNER // G)).reshape(b, t, D_INNER)
    return y, h_last.reshape(b, N_SSM_HEADS, P, N), new_conv


def memory_kv(mem, g, wk, wv):
    b = mem.shape[0]
    m = rms_norm(mem, g)
    k = (m @ wk).reshape(b, N_MEM, N_MEM_HEADS, MEM_HEAD_DIM)
    v = (m @ wv).reshape(b, N_MEM, N_MEM_HEADS, MEM_HEAD_DIM)
    return k, v


def cross_attention(h, mk, mv, wq, wo):
    b, t = h.shape[0], h.shape[1]
    q = (h @ wq).reshape(b, t, N_MEM_HEADS, MEM_HEAD_DIM).astype(jnp.float32)
    s = jnp.einsum('bqhd,bkhd->bhqk', q, mk.astype(jnp.float32)) * (MEM_HEAD_DIM ** -0.5)
    a = jax.nn.softmax(s, axis=-1)
    o = jnp.einsum('bhqk,bkhd->bqhd', a, mv.astype(jnp.float32)).reshape(b, t, D_MODEL)
    return o.astype(h.dtype) @ wo


def decoder_layer(x, k_past, v_past, ssm0, conv0, mem_k, mem_v, lp, lambda_init):
    b, t = x.shape[0], x.shape[1]
    p0 = k_past.shape[1]
    q_pos = p0 + jnp.arange(t, dtype=jnp.int32)
    k_pos = jnp.arange(p0 + t, dtype=jnp.int32)
    h = rms_norm(x, lp['g_pre_mix'])
    q, k, v, z, xbc, dt_raw, gate_a, gate_b = split_cols(h @ lp['w_in'], IN_SIZES)
    qh = q.reshape(b, t, N_DIFF_KV, GQA_REP, 2, HEAD_DIM)
    kh = k.reshape(b, t, N_DIFF_KV, 2, HEAD_DIM)
    vh = v.reshape(b, t, N_DIFF_KV, 2 * HEAD_DIM)
    k_all = jnp.concatenate([k_past.astype(kh.dtype), kh], axis=1)
    v_all = jnp.concatenate([v_past.astype(vh.dtype), vh], axis=1)
    lam = (jnp.exp(jnp.sum(lp['lambda_q1'].astype(jnp.float32) * lp['lambda_k1'].astype(jnp.float32)))
           - jnp.exp(jnp.sum(lp['lambda_q2'].astype(jnp.float32) * lp['lambda_k2'].astype(jnp.float32)))
           + lambda_init)
    attn_o = diff_attention(qh, k_all, v_all, q_pos, k_pos, lam, lp['diff_norm_g'], lambda_init)
    ssd_o, ssm_new, conv_new = ssd_branch(z, xbc, dt_raw, ssm0, conv0, lp)
    ya = attn_o.astype(x.dtype) @ lp['w_attn_branch']
    yb = ssd_o.astype(x.dtype) @ lp['w_ssd_branch']
    merged = jax.nn.sigmoid(gate_a) * ya + jax.nn.sigmoid(gate_b) * yb
    x = x + rms_norm(merged @ lp['w_mix_out'], lp['g_post_mix'])
    h = rms_norm(x, lp['g_pre_mem'])
    x = x + rms_norm(cross_attention(h, mem_k, mem_v, lp['w_mem_q'], lp['w_mem_o']), lp['g_post_mem'])
    h = rms_norm(x, lp['g_pre_mlp'])
    u = jnp.square(jax.nn.relu(h @ lp['w_up']))
    x = x + rms_norm(u @ lp['w_down'], lp['g_post_mlp'])
    return x, kh, vh, ssm_new, conv_new


def setup_inputs(seed: int = 0) -> dict:
    key = jax.random.key(seed)
    ks = iter(jax.random.split(key, 64))
    f32 = jnp.float32

    def nrm(shape, scale):
        return jax.random.normal(next(ks), shape, f32) * scale

    def gain(n):
        return 1.0 + nrm((DEPTH, n), 0.02)

    n_pages = PAST_LEN // PAGE_SIZE
    n_used = DEC_BATCH * n_pages
    n_pool = n_used + n_used // 4
    inp = {}
    inp['x_prompt'] = nrm((BATCH, SEQ, D_MODEL), 1.0)
    inp['x_sample'] = nrm((DEC_BATCH, DEC_SEQ, D_MODEL), 1.0)
    inp['cache_k'] = nrm((DEPTH, n_pool, PAGE_SIZE, N_DIFF_KV, 2, HEAD_DIM), 1.0)
    inp['cache_v'] = nrm((DEPTH, n_pool, PAGE_SIZE, N_DIFF_KV, 2 * HEAD_DIM), 1.0)
    inp['cache_mem_k'] = nrm((DEPTH, DEC_BATCH, N_MEM, N_MEM_HEADS, MEM_HEAD_DIM), 1.0)
    inp['cache_mem_v'] = nrm((DEPTH, DEC_BATCH, N_MEM, N_MEM_HEADS, MEM_HEAD_DIM), 1.0)
    inp['state_ssm'] = nrm((DEPTH, DEC_BATCH, N_SSM_HEADS, SSM_HEAD_DIM, D_STATE), 0.5)
    inp['state_conv'] = nrm((DEPTH, DEC_BATCH, CONV_WIDTH - 1, CONV_DIM), 1.0)
    inp['page_table'] = jax.random.permutation(next(ks), n_pool)[:n_used].reshape(DEC_BATCH, n_pages).astype(jnp.int32)
    inp['mem_prompt'] = nrm((BATCH, N_MEM, D_MODEL), 1.0)
    inp['g_pre_mix'] = gain(D_MODEL)
    inp['w_in'] = nrm((DEPTH, D_MODEL, IN_WIDTH), D_MODEL ** -0.5)
    inp['lambda_q1'] = nrm((DEPTH, HEAD_DIM), 0.1)
    inp['lambda_k1'] = nrm((DEPTH, HEAD_DIM), 0.1)
    inp['lambda_q2'] = nrm((DEPTH, HEAD_DIM), 0.1)
    inp['lambda_k2'] = nrm((DEPTH, HEAD_DIM), 0.1)
    inp['diff_norm_g'] = gain(2 * HEAD_DIM)
    inp['conv_w'] = nrm((DEPTH, CONV_WIDTH, CONV_DIM), CONV_WIDTH ** -0.5)
    inp['conv_b'] = nrm((DEPTH, CONV_DIM), 0.02)
    dt0 = jnp.exp(jax.random.uniform(next(ks), (DEPTH, N_SSM_HEADS), f32, math.log(1e-3), math.log(1e-1)))
    inp['dt_bias'] = dt0 + jnp.log(-jnp.expm1(-dt0))
    inp['a_log'] = jnp.log(jax.random.uniform(next(ks), (DEPTH, N_SSM_HEADS), f32, 1.0, 16.0))
    inp['d_skip'] = 1.0 + nrm((DEPTH, N_SSM_HEADS), 0.02)
    inp['ssm_norm_g'] = gain(D_INNER)
    inp['w_attn_branch'] = nrm((DEPTH, ATTN_OUT_WIDTH, D_MODEL), ATTN_OUT_WIDTH ** -0.5)
    inp['w_ssd_branch'] = nrm((DEPTH, D_INNER, D_MODEL), D_INNER ** -0.5)
    inp['w_mix_out'] = nrm((DEPTH, D_MODEL, D_MODEL), D_MODEL ** -0.5)
    inp['g_post_mix'] = gain(D_MODEL)
    inp['g_pre_mem'] = gain(D_MODEL)
    inp['mem_norm_g'] = gain(D_MODEL)
    inp['w_mem_q'] = nrm((DEPTH, D_MODEL, D_MODEL), D_MODEL ** -0.5)
    inp['w_mem_k'] = nrm((DEPTH, D_MODEL, D_MODEL), D_MODEL ** -0.5)
    inp['w_mem_v'] = nrm((DEPTH, D_MODEL, D_MODEL), D_MODEL ** -0.5)
    inp['w_mem_o'] = nrm((DEPTH, D_MODEL, D_MODEL), D_MODEL ** -0.5)
    inp['g_post_mem'] = gain(D_MODEL)
    inp['g_pre_mlp'] = gain(D_MODEL)
    inp['w_up'] = nrm((DEPTH, D_MODEL, D_FF), D_MODEL ** -0.5)
    inp['w_down'] = nrm((DEPTH, D_FF, D_MODEL), D_FF ** -0.5)
    inp['g_post_mlp'] = gain(D_MODEL)
    return inp


def reference(x_prompt, x_sample, cache_k, cache_v, cache_mem_k, cache_mem_v, state_ssm, state_conv, page_table,
              mem_prompt, g_pre_mix, w_in, lambda_q1, lambda_k1, lambda_q2, lambda_k2, diff_norm_g, conv_w, conv_b,
              dt_bias, a_log, d_skip, ssm_norm_g, w_attn_branch, w_ssd_branch, w_mix_out, g_post_mix, g_pre_mem,
              mem_norm_g, w_mem_q, w_mem_k, w_mem_v, w_mem_o, g_post_mem, g_pre_mlp, w_up, w_down, g_post_mlp):
    bp = x_prompt.shape[0]
    bs, n_pages = page_table.shape
    past_len = n_pages * PAGE_SIZE
    yp, ys = x_prompt, x_sample
    kp_l, vp_l, ks_l, vs_l, sp_l, cp_l, ss_l, cs_l, mk_l, mv_l = [], [], [], [], [], [], [], [], [], []
    for i in range(DEPTH):
        lp = dict(g_pre_mix=g_pre_mix[i], w_in=w_in[i], lambda_q1=lambda_q1[i], lambda_k1=lambda_k1[i],
                  lambda_q2=lambda_q2[i], lambda_k2=lambda_k2[i], diff_norm_g=diff_norm_g[i], conv_w=conv_w[i],
                  conv_b=conv_b[i], dt_bias=dt_bias[i], a_log=a_log[i], d_skip=d_skip[i], ssm_norm_g=ssm_norm_g[i],
                  w_attn_branch=w_attn_branch[i], w_ssd_branch=w_ssd_branch[i], w_mix_out=w_mix_out[i],
                  g_post_mix=g_post_mix[i], g_pre_mem=g_pre_mem[i], w_mem_q=w_mem_q[i], w_mem_o=w_mem_o[i],
                  g_post_mem=g_post_mem[i], g_pre_mlp=g_pre_mlp[i], w_up=w_up[i], w_down=w_down[i],
                  g_post_mlp=g_post_mlp[i])
        lambda_init = 0.8 - 0.6 * math.exp(-0.3 * i)
        mk_p, mv_p = memory_kv(mem_prompt, mem_norm_g[i], w_mem_k[i], w_mem_v[i])
        k0 = jnp.zeros((bp, 0, N_DIFF_KV, 2, HEAD_DIM), yp.dtype)
        v0 = jnp.zeros((bp, 0, N_DIFF_KV, 2 * HEAD_DIM), yp.dtype)
        ssm0 = jnp.zeros((bp, N_SSM_HEADS, SSM_HEAD_DIM, D_STATE), jnp.float32)
        conv0 = jnp.zeros((bp, CONV_WIDTH - 1, CONV_DIM), yp.dtype)
        yp, kp, vp, sp, cp = decoder_layer(yp, k0, v0, ssm0, conv0, mk_p, mv_p, lp, lambda_init)
        k_past = cache_k[i, page_table].reshape(bs, past_len, N_DIFF_KV, 2, HEAD_DIM)
        v_past = cache_v[i, page_table].reshape(bs, past_len, N_DIFF_KV, 2 * HEAD_DIM)
        ys, ksn, vsn, ssn, csn = decoder_layer(ys, k_past, v_past, state_ssm[i], state_conv[i],
                                               cache_mem_k[i], cache_mem_v[i], lp, lambda_init)
        kp_l.append(kp); vp_l.append(vp); ks_l.append(ksn); vs_l.append(vsn)
        sp_l.append(sp); cp_l.append(cp); ss_l.append(ssn); cs_l.append(csn)
        mk_l.append(mk_p); mv_l.append(mv_p)
    return (yp, ys, jnp.stack(kp_l), jnp.stack(vp_l), jnp.stack(ks_l), jnp.stack(vs_l), jnp.stack(sp_l),
            jnp.stack(cp_l), jnp.stack(ss_l), jnp.stack(cs_l), jnp.stack(mk_l), jnp.stack(mv_l))
```

```python
import functools
import math

import jax
import jax.numpy as jnp
import numpy as np
from jax import lax
from jax.experimental import pallas as pl
from jax.experimental.pallas import tpu as pltpu

F32 = jnp.float32
BF16 = jnp.bfloat16

D_MODEL = 1024
HEAD_DIM = 64
N_DIFF_HEADS = 8
N_DIFF_KV = 4
GQA_REP = 2
D_INNER = 2048
SSM_HEAD_DIM = 64
N_SSM_HEADS = 32
N_SSM_GROUPS = 4
D_STATE = 128
CONV_WIDTH = 4
CONV_DIM = 3072
SSD_CHUNK = 128
N_MEM_HEADS = 4
MEM_HEAD_DIM = 256
D_FF = 4096
EPS = 1e-6
PAGE_SIZE = 128

BIG_XBC, BIG_Q, BIG_Z, BIG_GA, BIG_GB = 0, 3072, 4096, 6144, 7168
BIG_WIDTH = 8192
DT_PAD = 128

VMEM_LIMIT_BYTES = 48 * 1024 * 1024

NT_DIMS = (((1,), (1,)), ((), ()))


def _cparams(n_axes):
    return pltpu.CompilerParams(dimension_semantics=("arbitrary",) * n_axes,
                                vmem_limit_bytes=VMEM_LIMIT_BYTES)


def _rms(x, g):
    ms = jnp.mean(x * x, axis=-1, keepdims=True)
    return x * lax.rsqrt(ms + EPS) * g


def _sigmoid(x):
    return 1.0 / (1.0 + jnp.exp(-x))


def _silu(x):
    return x * _sigmoid(x)


def _softplus(x):
    return jnp.maximum(x, 0.0) + jnp.log1p(jnp.exp(-jnp.abs(x)))


def _split3(x):
    hi = x.astype(BF16)
    r1 = x - hi.astype(F32)
    mid = r1.astype(BF16)
    lo = (r1 - mid.astype(F32)).astype(BF16)
    return hi, mid, lo


def _dot3(x, sel):
    hi, mid, lo = _split3(x)
    f = lambda a: jnp.dot(a, sel, preferred_element_type=F32)
    return f(hi) + f(mid) + f(lo)


def _sel_dot3(sel, x):
    hi, mid, lo = _split3(x)
    f = lambda a: jnp.dot(sel, a, preferred_element_type=F32)
    return f(hi) + f(mid) + f(lo)


def _norm_proj_kernel(*refs, has_big, n_small):
    x_ref, g_ref = refs[0], refs[1]
    idx = 2
    wb_ref = None
    if has_big:
        wb_ref = refs[idx]
        idx += 1
    ws = refs[idx:idx + n_small]
    idx += n_small
    ob_ref = None
    if has_big:
        ob_ref = refs[idx]
        idx += 1
    outs = refs[idx:idx + n_small]
    idx += n_small
    h_ref = refs[idx]

    @pl.when(pl.program_id(1) == 0)
    def _():
        h = _rms(x_ref[...], g_ref[...]).astype(BF16)
        h_ref[...] = h
        for w, o in zip(ws, outs):
            o[...] = jnp.dot(h, w[...], preferred_element_type=F32)

    if has_big:
        ob_ref[...] = jnp.dot(h_ref[...], wb_ref[...], preferred_element_type=F32).astype(BF16)


def norm_proj(x, g, w_big, w_smalls, *, tm=1024, tn=512):
    m, d = x.shape
    tm = min(tm, m)
    has_big = w_big is not None
    nb = w_big.shape[1] if has_big else 0
    tn = min(tn, nb) if has_big else 0
    nj = nb // tn if has_big else 1
    in_specs = [pl.BlockSpec((tm, d), lambda i, j: (i, 0)), pl.BlockSpec((1, d), lambda i, j: (0, 0))]
    args = [x, g.reshape(1, d)]
    out_shape, out_specs = [], []
    if has_big:
        in_specs.append(pl.BlockSpec((d, tn), lambda i, j: (0, j)))
        args.append(w_big)
        out_shape.append(jax.ShapeDtypeStruct((m, nb), BF16))
        out_specs.append(pl.BlockSpec((tm, tn), lambda i, j: (i, j)))
    for w in w_smalls:
        in_specs.append(pl.BlockSpec(w.shape, lambda i, j: (0, 0)))
        args.append(w)
    for w in w_smalls:
        out_shape.append(jax.ShapeDtypeStruct((m, w.shape[1]), F32))
        out_specs.append(pl.BlockSpec((tm, w.shape[1]), lambda i, j: (i, 0)))
    outs = pl.pallas_call(
        functools.partial(_norm_proj_kernel, has_big=has_big, n_small=len(w_smalls)),
        grid=(m // tm, nj),
        in_specs=in_specs,
        out_specs=out_specs,
        out_shape=out_shape,
        scratch_shapes=[pltpu.VMEM((tm, d), BF16)],
        compiler_params=_cparams(2),
        name="norm_proj",
    )(*args)
    big = outs[0] if has_big else None
    return big, list(outs[1:] if has_big else outs)


def _lambda_value(lamv_ref, lambda_init):
    lv = lamv_ref[...]
    s1 = jnp.sum(lv[0:1] * lv[1:2], axis=-1, keepdims=True)
    s2 = jnp.sum(lv[2:3] * lv[3:4], axis=-1, keepdims=True)
    return jnp.exp(s1) - jnp.exp(s2) + lambda_init


def _diff_attn_kernel(slopes_ref, q_ref, k_ref, v_ref, lamv_ref, subg_ref, o_ref, m_ref, l_ref, acc_ref, *,
                      tq, tk, lambda_init):
    kv = pl.program_id(1)
    qi = pl.program_id(2)
    ki = pl.program_id(3)

    @pl.when(ki == 0)
    def _():
        m_ref[...] = jnp.full(m_ref.shape, -jnp.inf, F32)
        l_ref[...] = jnp.zeros(l_ref.shape, F32)
        acc_ref[...] = jnp.zeros(acc_ref.shape, F32)

    def update(masked):
        k = k_ref[0].astype(BF16)
        v = v_ref[0].astype(BF16)
        kpos_rel = (ki * tk - qi * tq + lax.broadcasted_iota(jnp.int32, (1, tk), 1)).astype(F32)
        if masked:
            row = lax.broadcasted_iota(jnp.int32, (tq, tk), 0)
            col = lax.broadcasted_iota(jnp.int32, (tq, tk), 1)
            visible = col <= row
        for r in range(GQA_REP):
            bias = slopes_ref[kv * GQA_REP + r] * kpos_rel
            for i in range(2):
                c = r * 2 + i
                qh = q_ref[0, :, c * HEAD_DIM:(c + 1) * HEAD_DIM]
                s = lax.dot_general(qh, k[:, i * HEAD_DIM:(i + 1) * HEAD_DIM], NT_DIMS,
                                    preferred_element_type=F32) * (HEAD_DIM ** -0.5) + bias
                if masked:
                    s = jnp.where(visible, s, -jnp.inf)
                m_prev = m_ref[c]
                m_new = jnp.maximum(m_prev, jnp.max(s, axis=-1, keepdims=True))
                alpha = jnp.exp(m_prev - m_new)
                p = jnp.exp(s - m_new)
                l_ref[c] = alpha * l_ref[c] + jnp.sum(p, axis=-1, keepdims=True)
                acc_ref[c] = alpha * acc_ref[c] + jnp.dot(p.astype(BF16), v, preferred_element_type=F32)
                m_ref[c] = m_new

    @pl.when(ki < qi)
    def _():
        update(False)

    @pl.when(ki == qi)
    def _():
        update(True)
        lam = _lambda_value(lamv_ref, lambda_init)
        for r in range(GQA_REP):
            o0 = acc_ref[2 * r] / l_ref[2 * r]
            o1 = acc_ref[2 * r + 1] / l_ref[2 * r + 1]
            d = _rms(o0 - lam * o1, subg_ref[...]) * (1.0 - lambda_init)
            o_ref[0, :, r * 128:(r + 1) * 128] = d.astype(o_ref.dtype)


def diff_attn_prompt(big, k, v, slopes, lamv, subg, lambda_init, *, tile=512):
    b, t, _ = big.shape
    tq = tk = min(tile, t)
    nq = t // tq
    q_blk0 = BIG_Q // 256
    grid_spec = pltpu.PrefetchScalarGridSpec(
        num_scalar_prefetch=1,
        grid=(b, N_DIFF_KV, nq, nq),
        in_specs=[
            pl.BlockSpec((1, tq, 256), lambda bi, kv, qi, ki, sl: (bi, qi, q_blk0 + kv)),
            pl.BlockSpec((1, tk, 128), lambda bi, kv, qi, ki, sl: (bi, jnp.minimum(ki, qi), kv)),
            pl.BlockSpec((1, tk, 128), lambda bi, kv, qi, ki, sl: (bi, jnp.minimum(ki, qi), kv)),
            pl.BlockSpec((4, HEAD_DIM), lambda bi, kv, qi, ki, sl: (0, 0)),
            pl.BlockSpec((1, 128), lambda bi, kv, qi, ki, sl: (0, 0)),
        ],
        out_specs=pl.BlockSpec((1, tq, 256), lambda bi, kv, qi, ki, sl: (bi, qi, kv)),
        scratch_shapes=[pltpu.VMEM((4, tq, 1), F32), pltpu.VMEM((4, tq, 1), F32), pltpu.VMEM((4, tq, 128), F32)],
    )
    return pl.pallas_call(
        functools.partial(_diff_attn_kernel, tq=tq, tk=tk, lambda_init=lambda_init),
        grid_spec=grid_spec,
        out_shape=jax.ShapeDtypeStruct((b, t, 1024), BF16),
        compiler_params=_cparams(4),
        name="diff_attn_prompt",
    )(slopes, big, k, v, lamv, subg)


def _diff_attn_paged_kernel(pt_ref, qbd_ref, knew_ref, vnew_ref, slope_ref, lamv_ref, subg_ref, *refs,
                            pages_per_step, past_len, lambda_init):
    del pt_ref
    k_refs = refs[:pages_per_step]
    v_refs = refs[pages_per_step:2 * pages_per_step]
    o_ref, m_ref, l_ref, acc_ref = refs[2 * pages_per_step:]
    j = pl.program_id(1)
    qbd = qbd_ref[0]

    @pl.when(j == 0)
    def _():
        s_self = jnp.sum(qbd.astype(F32) * knew_ref[0], axis=-1, keepdims=True)
        m_ref[...] = s_self
        l_ref[...] = jnp.ones(l_ref.shape, F32)
        acc_ref[...] = jnp.broadcast_to(vnew_ref[0], acc_ref.shape)

    tk = pages_per_step * PAGE_SIZE
    s = jnp.concatenate(
        [lax.dot_general(qbd, kr[...].astype(BF16), NT_DIMS, preferred_element_type=F32) for kr in k_refs], axis=-1)
    kpos = (j * tk + lax.broadcasted_iota(jnp.int32, (1, tk), 1) - past_len).astype(F32)
    s = s + slope_ref[...] * kpos
    m_prev = m_ref[...]
    m_new = jnp.maximum(m_prev, jnp.max(s, axis=-1, keepdims=True))
    alpha = jnp.exp(m_prev - m_new)
    p = jnp.exp(s - m_new)
    l_ref[...] = alpha * l_ref[...] + jnp.sum(p, axis=-1, keepdims=True)
    pb = p.astype(BF16)
    pv = jnp.dot(pb[:, 0:PAGE_SIZE], v_refs[0][...].astype(BF16), preferred_element_type=F32)
    for n in range(1, pages_per_step):
        pv = pv + jnp.dot(pb[:, n * PAGE_SIZE:(n + 1) * PAGE_SIZE], v_refs[n][...].astype(BF16),
                          preferred_element_type=F32)
    acc_ref[...] = alpha * acc_ref[...] + pv
    m_ref[...] = m_new

    @pl.when(j == pl.num_programs(1) - 1)
    def _():
        lam = _lambda_value(lamv_ref, lambda_init)
        o = acc_ref[...] / l_ref[...]
        for kv in range(N_DIFF_KV):
            blk = o[kv * 4:(kv + 1) * 4, kv * 128:(kv + 1) * 128]
            for r in range(GQA_REP):
                d = blk[2 * r:2 * r + 1] - lam * blk[2 * r + 1:2 * r + 2]
                d = _rms(d, subg_ref[...]) * (1.0 - lambda_init)
                c = kv * GQA_REP + r
                o_ref[0, :, c * 128:(c + 1) * 128] = d.astype(o_ref.dtype)


def diff_attn_paged(qbd, k_new, v_new, cache_k, cache_v, page_table, slope_rows, lamv, subg, lambda_init,
                    *, pages_per_step=8):
    bs, n_pages = page_table.shape
    pps = min(pages_per_step, n_pages)
    nsteps = n_pages // pps

    def page_spec(n):
        return pl.BlockSpec((None, PAGE_SIZE, 512), lambda b, j, pt: (pt[b, j * pps + n], 0, 0))

    grid_spec = pltpu.PrefetchScalarGridSpec(
        num_scalar_prefetch=1,
        grid=(bs, nsteps),
        in_specs=[
            pl.BlockSpec((1, 16, 512), lambda b, j, pt: (b, 0, 0)),
            pl.BlockSpec((1, 1, 512), lambda b, j, pt: (b, 0, 0)),
            pl.BlockSpec((1, 1, 512), lambda b, j, pt: (b, 0, 0)),
            pl.BlockSpec((16, 1), lambda b, j, pt: (0, 0)),
            pl.BlockSpec((4, HEAD_DIM), lambda b, j, pt: (0, 0)),
            pl.BlockSpec((1, 128), lambda b, j, pt: (0, 0)),
        ] + [page_spec(n) for n in range(pps)] + [page_spec(n) for n in range(pps)],
        out_specs=pl.BlockSpec((1, 1, 1024), lambda b, j, pt: (b, 0, 0)),
        scratch_shapes=[pltpu.VMEM((16, 1), F32), pltpu.VMEM((16, 1), F32), pltpu.VMEM((16, 512), F32)],
    )
    return pl.pallas_call(
        functools.partial(_diff_attn_paged_kernel, pages_per_step=pps, past_len=n_pages * PAGE_SIZE,
                          lambda_init=lambda_init),
        grid_spec=grid_spec,
        out_shape=jax.ShapeDtypeStruct((bs, 1, 1024), F32),
        compiler_params=_cparams(2),
        name="diff_attn_paged",
    )(page_table, qbd, k_new, v_new, slope_rows, lamv, subg, *([cache_k] * pps), *([cache_v] * pps))


def _ssd_prompt_kernel(xbc_ref, z_ref, dt_ref, convw_ref, convb_ref, dtb_ref, alog_ref, dskip_ref, ng_ref,
                       expand_ref, tril_ref, y_ref, st_out_ref, conv_out_ref, xbuf_ref, st_ref):
    c = pl.program_id(1)
    L = SSD_CHUNK
    GW = D_INNER // N_SSM_GROUPS

    @pl.when(c == 0)
    def _():
        xbuf_ref[0:8, :] = jnp.zeros((8, CONV_DIM), F32)
        st_ref[...] = jnp.zeros(st_ref.shape, F32)

    xc = xbc_ref[0].astype(F32)
    xbuf_ref[8:8 + L, :] = xc
    w = convw_ref[...]
    conv = (convb_ref[...] + xbuf_ref[5:5 + L, :] * w[0:1] + xbuf_ref[6:6 + L, :] * w[1:2]
            + xbuf_ref[7:7 + L, :] * w[2:3] + xc * w[3:4])
    tail = xc[L - 3:L, :]
    xbuf_ref[5:8, :] = tail

    @pl.when(c == pl.num_programs(1) - 1)
    def _():
        conv_out_ref[0] = tail

    act = _silu(conv)
    xs = act[:, :D_INNER]
    xs_b = xs.astype(BF16)

    lane = lax.broadcasted_iota(jnp.int32, (L, DT_PAD), 1)
    dt = jnp.where(lane < N_SSM_HEADS, _softplus(dt_ref[0] + dtb_ref[...]), 0.0)
    d_a = dt * (-jnp.exp(alog_ref[...]))
    tril = tril_ref[...]
    a_cs = _sel_dot3(tril, d_a)
    a_cs_t = a_cs.T
    dt_t = dt.T
    a_end = a_cs[L - 1:L, :]
    expand = expand_ref[...]
    ea_x = _dot3(jnp.exp(a_cs), expand)
    w_x = _dot3(jnp.exp(a_end - a_cs) * dt, expand)
    xw_b = (xs * w_x).astype(BF16)

    row = lax.broadcasted_iota(jnp.int32, (L, L), 0)
    col = lax.broadcasted_iota(jnp.int32, (L, L), 1)
    causal = row >= col
    half = lax.broadcasted_iota(jnp.int32, (L, 128), 1) < SSM_HEAD_DIM

    y_parts = []
    for g in range(N_SSM_GROUPS):
        bg = act[:, D_INNER + g * D_STATE:D_INNER + (g + 1) * D_STATE]
        cg = act[:, D_INNER + N_SSM_GROUPS * D_STATE + g * D_STATE:D_INNER + N_SSM_GROUPS * D_STATE + (g + 1) * D_STATE]
        bg_b = bg.astype(BF16)
        cg_b = cg.astype(BF16)
        cb = lax.dot_general(cg_b, bg_b, NT_DIMS, preferred_element_type=F32)
        st_g = st_ref[:, g * GW:(g + 1) * GW]
        y_off = jnp.dot(cg_b, st_g.astype(BF16), preferred_element_type=F32) * ea_x[:, g * GW:(g + 1) * GW]
        y_diag_parts = []
        for pair in range(GW // 128):
            h0 = g * (GW // SSM_HEAD_DIM) + 2 * pair
            x_pair = xs_b[:, g * GW + pair * 128:g * GW + (pair + 1) * 128]
            ys = []
            for h in (h0, h0 + 1):
                diff = a_cs[:, h:h + 1] - a_cs_t[h:h + 1, :]
                decay = jnp.exp(jnp.where(causal, diff, -jnp.inf))
                mh = (cb * decay * dt_t[h:h + 1, :]).astype(BF16)
                ys.append(jnp.dot(mh, x_pair, preferred_element_type=F32))
            y_diag_parts.append(jnp.where(half, ys[0], ys[1]))
        y_parts.append(jnp.concatenate(y_diag_parts, axis=-1) + y_off)
        new_g = jnp.dot(bg.T.astype(BF16), xw_b[:, g * GW:(g + 1) * GW], preferred_element_type=F32)
        st_ref[:, g * GW:(g + 1) * GW] = st_g * ea_x[L - 1:L, g * GW:(g + 1) * GW] + new_g

    y = jnp.concatenate(y_parts, axis=-1) + dskip_ref[...] * xs
    y = y * _silu(z_ref[0].astype(F32))
    ng = ng_ref[...]
    for g in range(N_SSM_GROUPS):
        y_ref[0, :, g * GW:(g + 1) * GW] = _rms(y[:, g * GW:(g + 1) * GW], ng[:, g * GW:(g + 1) * GW]).astype(y_ref.dtype)

    @pl.when(c == pl.num_programs(1) - 1)
    def _():
        st_out_ref[0] = st_ref[...].T


def ssd_prompt(big, dt_raw, conv_w, conv_b, dt_bias, a_log, d_skip_x, norm_g, expand, tril):
    b, t, _ = big.shape
    nc = t // SSD_CHUNK
    const = lambda shape: pl.BlockSpec(shape, lambda bi, ci: (0,) * len(shape))
    return pl.pallas_call(
        _ssd_prompt_kernel,
        grid=(b, nc),
        in_specs=[
            pl.BlockSpec((1, SSD_CHUNK, CONV_DIM), lambda bi, ci: (bi, ci, BIG_XBC // CONV_DIM)),
            pl.BlockSpec((1, SSD_CHUNK, D_INNER), lambda bi, ci: (bi, ci, BIG_Z // D_INNER)),
            pl.BlockSpec((1, SSD_CHUNK, DT_PAD), lambda bi, ci: (bi, ci, 0)),
            const((CONV_WIDTH, CONV_DIM)), const((1, CONV_DIM)), const((1, DT_PAD)), const((1, DT_PAD)),
            const((1, D_INNER)), const((1, D_INNER)), const((DT_PAD, D_INNER)), const((SSD_CHUNK, SSD_CHUNK)),
        ],
        out_specs=[
            pl.BlockSpec((1, SSD_CHUNK, D_INNER), lambda bi, ci: (bi, ci, 0)),
            pl.BlockSpec((1, D_INNER, D_STATE), lambda bi, ci: (bi, 0, 0)),
            pl.BlockSpec((1, CONV_WIDTH - 1, CONV_DIM), lambda bi, ci: (bi, 0, 0)),
        ],
        out_shape=[
            jax.ShapeDtypeStruct((b, t, D_INNER), BF16),
            jax.ShapeDtypeStruct((b, D_INNER, D_STATE), F32),
            jax.ShapeDtypeStruct((b, CONV_WIDTH - 1, CONV_DIM), F32),
        ],
        scratch_shapes=[pltpu.VMEM((8 + SSD_CHUNK, CONV_DIM), F32), pltpu.VMEM((D_STATE, D_INNER), F32)],
        compiler_params=_cparams(2),
        name="ssd_prompt",
    )(big, big, dt_raw, conv_w, conv_b, dt_bias, a_log, d_skip_x, norm_g, expand, tril)


def _ssd_step_pre_kernel(xbc_ref, c0_ref, c1_ref, c2_ref, dt_ref, convw_ref, convb_ref, dtb_ref, alog_ref, dskip_ref,
                         expand_ref, gsum_ref, dtx_ref, cd_ref, ypre_ref, b_ref, c_ref):
    xnew = xbc_ref[...].astype(F32)
    w = convw_ref[...]
    conv = (convb_ref[...] + c0_ref[...] * w[0:1] + c1_ref[...] * w[1:2] + c2_ref[...] * w[2:3] + xnew * w[3:4])
    act = _silu(conv)
    xs = act[:, :D_INNER]
    bm = act[:, D_INNER:D_INNER + N_SSM_GROUPS * D_STATE]
    cm = act[:, D_INNER + N_SSM_GROUPS * D_STATE:]
    lane = lax.broadcasted_iota(jnp.int32, dt_ref.shape, 1)
    dt = jnp.where(lane < N_SSM_HEADS, _softplus(dt_ref[...] + dtb_ref[...]), 0.0)
    d_a = dt * (-jnp.exp(alog_ref[...]))
    expand = expand_ref[...]
    dt_x = _dot3(dt, expand)
    cd_x = _dot3(jnp.exp(d_a), expand)
    dtx = dt_x * xs
    cb_x = _dot3(cm.astype(BF16).astype(F32) * bm.astype(BF16).astype(F32), gsum_ref[...])
    dtx_ref[...] = dtx
    cd_ref[...] = cd_x
    ypre_ref[...] = cb_x * dtx + dskip_ref[...] * xs
    b_ref[...] = bm
    c_ref[...] = cm


def _ssd_step_state_kernel(st_ref, dtxt_ref, cdt_ref, cd_ref, ypre_ref, z_ref, b_ref, c_ref, ng_ref, stout_ref, y_ref):
    dtx_t = dtxt_ref[0]
    cd_t = cdt_ref[0]
    rows_per_group = D_INNER // N_SSM_GROUPS
    yoff_rows = []
    for j in range(D_INNER // 128):
        g = (j * 128) // rows_per_group
        h0 = st_ref[0, j * 128:(j + 1) * 128, :]
        bg = b_ref[0, g:g + 1, :]
        cg = c_ref[0, g:g + 1, :]
        stout_ref[0, j * 128:(j + 1) * 128, :] = h0 * cd_t[:, j:j + 1] + dtx_t[:, j:j + 1] * bg
        cg8 = jnp.broadcast_to(cg, (8, D_STATE)).astype(BF16)
        yo = lax.dot_general(cg8, h0.astype(BF16), NT_DIMS, preferred_element_type=F32)
        yoff_rows.append(yo[0:1])
    y = ypre_ref[0] + jnp.concatenate(yoff_rows, axis=0) * cd_ref[0]
    y = y * _silu(z_ref[0].astype(F32))
    ng = ng_ref[...]
    rpg = 16 // N_SSM_GROUPS
    for g in range(N_SSM_GROUPS):
        yg = y[g * rpg:(g + 1) * rpg]
        ms = jnp.sum(jnp.sum(yg * yg, axis=-1, keepdims=True), axis=0, keepdims=True) / (rpg * 128)
        y_ref[0, g * rpg:(g + 1) * rpg, :] = (yg * lax.rsqrt(ms + EPS) * ng[g * rpg:(g + 1) * rpg]).astype(y_ref.dtype)


def ssd_sample(big, dt_raw, state, conv0, conv_w, conv_b, dt_bias, a_log, d_skip_x, norm_g, expand, gsum):
    bs = big.shape[0]
    full = lambda shape: pl.BlockSpec(shape, lambda i: (0,) * len(shape))
    dtx, cd, ypre, bm, cm = pl.pallas_call(
        _ssd_step_pre_kernel,
        grid=(1,),
        in_specs=[
            pl.BlockSpec((bs, CONV_DIM), lambda i: (0, BIG_XBC // CONV_DIM)),
            full((bs, CONV_DIM)), full((bs, CONV_DIM)), full((bs, CONV_DIM)),
            full((bs, DT_PAD)), full((CONV_WIDTH, CONV_DIM)),
            full((1, CONV_DIM)), full((1, DT_PAD)), full((1, DT_PAD)), full((1, D_INNER)),
            full((DT_PAD, D_INNER)), full((N_SSM_GROUPS * D_STATE, D_INNER)),
        ],
        out_specs=[full((bs, D_INNER)), full((bs, D_INNER)),
                   full((bs, D_INNER)), full((bs, N_SSM_GROUPS * D_STATE)), full((bs, N_SSM_GROUPS * D_STATE))],
        out_shape=[
            jax.ShapeDtypeStruct((bs, D_INNER), F32), jax.ShapeDtypeStruct((bs, D_INNER), F32),
            jax.ShapeDtypeStruct((bs, D_INNER), F32),
            jax.ShapeDtypeStruct((bs, N_SSM_GROUPS * D_STATE), F32),
            jax.ShapeDtypeStruct((bs, N_SSM_GROUPS * D_STATE), F32),
        ],
        compiler_params=_cparams(1),
        name="ssd_step_pre",
    )(big, conv0[:, 0], conv0[:, 1], conv0[:, 2], dt_raw, conv_w, conv_b, dt_bias, a_log, d_skip_x, expand, gsum)
    newconv = jnp.concatenate([conv0[:, 1:], big[:, BIG_XBC:BIG_XBC + CONV_DIM].astype(F32)[:, None]], axis=1)

    z3 = big[:, BIG_Z:BIG_Z + D_INNER].reshape(bs, 16, 128)
    r3 = lambda a: a.reshape(bs, 16, 128)
    t3 = lambda a: jnp.swapaxes(r3(a), 1, 2)
    g3 = lambda a: a.reshape(bs, N_SSM_GROUPS, D_STATE)
    per_seq = lambda shape: pl.BlockSpec((1,) + shape, lambda i: (i, 0, 0))
    st_new, y = pl.pallas_call(
        _ssd_step_state_kernel,
        grid=(bs,),
        in_specs=[per_seq((D_INNER, D_STATE)), per_seq((128, 16)), per_seq((128, 16)), per_seq((16, 128)),
                  per_seq((16, 128)), per_seq((16, 128)), per_seq((N_SSM_GROUPS, D_STATE)),
                  per_seq((N_SSM_GROUPS, D_STATE)), pl.BlockSpec((16, 128), lambda i: (0, 0))],
        out_specs=[per_seq((D_INNER, D_STATE)), per_seq((16, 128))],
        out_shape=[jax.ShapeDtypeStruct((bs, D_INNER, D_STATE), F32), jax.ShapeDtypeStruct((bs, 16, 128), BF16)],
        compiler_params=_cparams(1),
        name="ssd_step_state",
    )(state, t3(dtx), t3(cd), r3(cd), r3(ypre), z3, g3(bm), g3(cm), norm_g.reshape(16, 128))
    return y.reshape(bs, D_INNER), st_new, newconv


def _mix_kernel(attn_ref, ssd_ref, ga_ref, gb_ref, x_ref, wa_ref, wb_ref, wm_ref, g_ref, o_ref):
    ya = jnp.dot(attn_ref[...], wa_ref[...], preferred_element_type=F32)
    yb = jnp.dot(ssd_ref[...], wb_ref[...], preferred_element_type=F32)
    merged = _sigmoid(ga_ref[...].astype(F32)) * ya + _sigmoid(gb_ref[...].astype(F32)) * yb
    mo = jnp.dot(merged.astype(BF16), wm_ref[...], preferred_element_type=F32)
    o_ref[...] = x_ref[...] + _rms(mo, g_ref[...])


def mix(attn, ssd_y, big, x, wa, wb, wm, g, *, tm=512):
    m = x.shape[0]
    tm = min(tm, m)
    full = lambda a: pl.BlockSpec(a.shape, lambda i: (0, 0))
    return pl.pallas_call(
        _mix_kernel,
        grid=(m // tm,),
        in_specs=[
            pl.BlockSpec((tm, 1024), lambda i: (i, 0)),
            pl.BlockSpec((tm, D_INNER), lambda i: (i, 0)),
            pl.BlockSpec((tm, 1024), lambda i: (i, BIG_GA // 1024)),
            pl.BlockSpec((tm, 1024), lambda i: (i, BIG_GB // 1024)),
            pl.BlockSpec((tm, D_MODEL), lambda i: (i, 0)),
            full(wa), full(wb), full(wm), pl.BlockSpec((1, D_MODEL), lambda i: (0, 0)),
        ],
        out_specs=pl.BlockSpec((tm, D_MODEL), lambda i: (i, 0)),
        out_shape=jax.ShapeDtypeStruct((m, D_MODEL), F32),
        compiler_params=_cparams(1),
        name="mix",
    )(attn, ssd_y, big, big, x, wa, wb, wm, g.reshape(1, D_MODEL))


def _cross_attn_kernel(q_ref, mk_ref, mv_ref, o_ref, *, tq):
    q = q_ref[0]
    rows = max(tq, 8)
    if tq < 8:
        q = jnp.broadcast_to(q.astype(F32), (rows, D_MODEL)).astype(BF16)
    for h in range(N_MEM_HEADS):
        sl = slice(h * MEM_HEAD_DIM, (h + 1) * MEM_HEAD_DIM)
        kh = mk_ref[0, :, sl].astype(BF16)
        vh = mv_ref[0, :, sl].astype(BF16)
        s = lax.dot_general(q[:, sl], kh, NT_DIMS, preferred_element_type=F32) * (MEM_HEAD_DIM ** -0.5)
        p = jnp.exp(s - jnp.max(s, axis=-1, keepdims=True))
        a = p / jnp.sum(p, axis=-1, keepdims=True)
        oh = jnp.dot(a.astype(BF16), vh, preferred_element_type=F32)
        o_ref[0, :, sl] = oh[0:tq].astype(o_ref.dtype)


def cross_attn(q, mk, mv, *, tq=512):
    b, t, _ = q.shape
    tq = min(tq, t)
    n_mem = mk.shape[1]
    return pl.pallas_call(
        functools.partial(_cross_attn_kernel, tq=tq),
        grid=(b, t // tq),
        in_specs=[
            pl.BlockSpec((1, tq, D_MODEL), lambda bi, ti: (bi, ti, 0)),
            pl.BlockSpec((1, n_mem, D_MODEL), lambda bi, ti: (bi, 0, 0)),
            pl.BlockSpec((1, n_mem, D_MODEL), lambda bi, ti: (bi, 0, 0)),
        ],
        out_specs=pl.BlockSpec((1, tq, D_MODEL), lambda bi, ti: (bi, ti, 0)),
        out_shape=jax.ShapeDtypeStruct((b, t, D_MODEL), BF16),
        compiler_params=_cparams(2),
        name="cross_attn",
    )(q, mk, mv)


def _proj_norm_res_kernel(a_ref, w_ref, g_ref, res_ref, o_ref):
    y = jnp.dot(a_ref[...], w_ref[...], preferred_element_type=F32)
    o_ref[...] = res_ref[...] + _rms(y, g_ref[...])


def proj_norm_res(a, w, g, res, *, tm=1024):
    m, k = a.shape
    tm = min(tm, m)
    return pl.pallas_call(
        _proj_norm_res_kernel,
        grid=(m // tm,),
        in_specs=[pl.BlockSpec((tm, k), lambda i: (i, 0)), pl.BlockSpec(w.shape, lambda i: (0, 0)),
                  pl.BlockSpec((1, D_MODEL), lambda i: (0, 0)), pl.BlockSpec((tm, D_MODEL), lambda i: (i, 0))],
        out_specs=pl.BlockSpec((tm, D_MODEL), lambda i: (i, 0)),
        out_shape=jax.ShapeDtypeStruct((m, D_MODEL), F32),
        compiler_params=_cparams(1),
        name="proj_norm_res",
    )(a, w, g.reshape(1, D_MODEL), res)


def _mlp_kernel(x_ref, gpre_ref, wup_ref, wdown_ref, gpost_ref, o_ref, h_ref, acc_ref):
    j = pl.program_id(1)

    @pl.when(j == 0)
    def _():
        h_ref[...] = _rms(x_ref[...], gpre_ref[...]).astype(BF16)
        acc_ref[...] = jnp.zeros(acc_ref.shape, F32)

    u = jnp.maximum(jnp.dot(h_ref[...], wup_ref[...], preferred_element_type=F32), 0.0)
    acc_ref[...] += jnp.dot((u * u).astype(BF16), wdown_ref[...], preferred_element_type=F32)

    @pl.when(j == pl.num_programs(1) - 1)
    def _():
        o_ref[...] = x_ref[...] + _rms(acc_ref[...], gpost_ref[...])


def mlp(x, g_pre, w_up, w_down, g_post, *, tm=1024, tf=512):
    m = x.shape[0]
    tm = min(tm, m)
    return pl.pallas_call(
        _mlp_kernel,
        grid=(m // tm, D_FF // tf),
        in_specs=[
            pl.BlockSpec((tm, D_MODEL), lambda i, j: (i, 0)),
            pl.BlockSpec((1, D_MODEL), lambda i, j: (0, 0)),
            pl.BlockSpec((D_MODEL, tf), lambda i, j: (0, j)),
            pl.BlockSpec((tf, D_MODEL), lambda i, j: (j, 0)),
            pl.BlockSpec((1, D_MODEL), lambda i, j: (0, 0)),
        ],
        out_specs=pl.BlockSpec((tm, D_MODEL), lambda i, j: (i, 0)),
        out_shape=jax.ShapeDtypeStruct((m, D_MODEL), F32),
        scratch_shapes=[pltpu.VMEM((tm, D_MODEL), BF16), pltpu.VMEM((tm, D_MODEL), F32)],
        compiler_params=_cparams(2),
        name="mlp",
    )(x, g_pre.reshape(1, D_MODEL), w_up, w_down, g_post.reshape(1, D_MODEL))


def _alibi_slopes():
    return np.asarray(2.0 ** (-8.0 * np.arange(1, N_DIFF_HEADS + 1) / N_DIFF_HEADS), np.float32)


def _post_mixer(x1, mk, mv, w, b, t):
    qc, _ = norm_proj(x1, w["g_pre_mem"], w["w_mem_q"], [], tn=1024)
    oc = cross_attn(qc.reshape(b, t, D_MODEL), mk, mv)
    x2 = proj_norm_res(oc.reshape(b * t, D_MODEL), w["w_mem_o"], w["g_post_mem"], x1)
    return mlp(x2, w["g_pre_mlp"], w["w_up"], w["w_down"], w["g_post_mlp"])


def kernel(x_prompt, x_sample, cache_k, cache_v, cache_mem_k, cache_mem_v, state_ssm, state_conv, page_table, mem_prompt, g_pre_mix, w_in, lambda_q1, lambda_k1, lambda_q2, lambda_k2, diff_norm_g, conv_w, conv_b, dt_bias, a_log, d_skip, ssm_norm_g, w_attn_branch, w_ssd_branch, w_mix_out, g_post_mix, g_pre_mem, mem_norm_g, w_mem_q, w_mem_k, w_mem_v, w_mem_o, g_post_mem, g_pre_mlp, w_up, w_down, g_post_mlp):
    depth = w_in.shape[0]
    assert depth == 1
    bp, t, _ = x_prompt.shape
    bs, n_pages = page_table.shape
    n_mem = mem_prompt.shape[1]
    n_pool = cache_k.shape[1]
    lambda_init = 0.8 - 0.6 * math.exp(-0.3 * 0)

    wi = w_in[0]
    offs = np.cumsum([0, 1024, 512, 512, D_INNER, CONV_DIM, N_SSM_HEADS, D_MODEL, D_MODEL])
    seg = lambda n: wi[:, offs[n]:offs[n + 1]]
    w_big = jnp.concatenate([seg(4), seg(0), seg(3), seg(6), seg(7)], axis=1).astype(BF16)
    w_k = seg(1).astype(BF16)
    w_v = seg(2).astype(BF16)
    w_dt = jnp.pad(seg(5), ((0, 0), (0, DT_PAD - N_SSM_HEADS))).astype(BF16)
    pad_heads = lambda a: jnp.pad(a.reshape(1, N_SSM_HEADS), ((0, 0), (0, DT_PAD - N_SSM_HEADS)))
    w = dict(
        g_pre_mem=g_pre_mem[0], w_mem_q=w_mem_q[0].astype(BF16), w_mem_o=w_mem_o[0].astype(BF16),
        g_post_mem=g_post_mem[0], g_pre_mlp=g_pre_mlp[0], w_up=w_up[0].astype(BF16),
        w_down=w_down[0].astype(BF16), g_post_mlp=g_post_mlp[0])
    wa = w_attn_branch[0].astype(BF16)
    wb = w_ssd_branch[0].astype(BF16)
    wm = w_mix_out[0].astype(BF16)
    lamv = jnp.stack([lambda_q1[0], lambda_k1[0], lambda_q2[0], lambda_k2[0]])
    subg = diff_norm_g[0].reshape(1, 2 * HEAD_DIM)
    slopes = jnp.asarray(_alibi_slopes())
    convw = conv_w[0]
    convb = conv_b[0].reshape(1, CONV_DIM)
    dtb = pad_heads(dt_bias[0])
    alog = pad_heads(a_log[0])
    d_skip_x = jnp.repeat(d_skip[0], SSM_HEAD_DIM).reshape(1, D_INNER)
    norm_g = ssm_norm_g[0].reshape(1, D_INNER)
    expand = jnp.asarray(np.kron(np.eye(DT_PAD, N_SSM_HEADS, dtype=np.float32),
                                 np.ones((1, SSM_HEAD_DIM), np.float32)), BF16)
    tril = jnp.asarray(np.tril(np.ones((SSD_CHUNK, SSD_CHUNK), np.float32)), BF16)
    gsum = jnp.asarray(np.kron(np.eye(N_SSM_GROUPS, dtype=np.float32),
                               np.ones((D_STATE, D_INNER // N_SSM_GROUPS), np.float32)), BF16)

    mp = bp * t
    xp = x_prompt.reshape(mp, D_MODEL)
    big_p, (k_p, v_p, dt_p) = norm_proj(xp, g_pre_mix[0], w_big, [w_k, w_v, w_dt])
    big_p3 = big_p.reshape(bp, t, BIG_WIDTH)
    attn_p = diff_attn_prompt(big_p3, k_p.reshape(bp, t, 512), v_p.reshape(bp, t, 512), slopes, lamv, subg,
                              lambda_init)
    ssd_y_p, ssm_p, conv_p = ssd_prompt(big_p3, dt_p.reshape(bp, t, DT_PAD), convw, convb, dtb, alog, d_skip_x,
                                        norm_g, expand, tril)
    x1_p = mix(attn_p.reshape(mp, 1024), ssd_y_p.reshape(mp, D_INNER), big_p, xp, wa, wb, wm, g_post_mix[0])
    _, (mk_p, mv_p) = norm_proj(mem_prompt.reshape(bp * n_mem, D_MODEL), mem_norm_g[0], None,
                                [w_mem_k[0].astype(BF16), w_mem_v[0].astype(BF16)], tm=512)
    y_p = _post_mixer(x1_p, mk_p.reshape(bp, n_mem, D_MODEL), mv_p.reshape(bp, n_mem, D_MODEL), w, bp, t)

    xs_ = x_sample.reshape(bs, D_MODEL)
    big_s, (k_s, v_s, dt_s) = norm_proj(xs_, g_pre_mix[0], w_big, [w_k, w_v, w_dt])
    q5 = big_s[:, BIG_Q:BIG_Q + 1024].reshape(bs, N_DIFF_KV, GQA_REP, 2, HEAD_DIM) * (HEAD_DIM ** -0.5)
    eye_kv = jnp.eye(N_DIFF_KV, dtype=BF16)
    eye_h = jnp.eye(2, dtype=BF16)
    qbd = (q5[:, :, :, :, None, None, :] * eye_kv[None, :, None, None, :, None, None]
           * eye_h[None, None, None, :, None, :, None]).reshape(bs, 16, 512)
    slope_rows = jnp.asarray(np.repeat(_alibi_slopes(), 2).reshape(16, 1))
    attn_s = diff_attn_paged(qbd, k_s.reshape(bs, 1, 512), v_s.reshape(bs, 1, 512),
                             cache_k.reshape(depth * n_pool, PAGE_SIZE, 512),
                             cache_v.reshape(depth * n_pool, PAGE_SIZE, 512), page_table, slope_rows, lamv, subg,
                             lambda_init)
    ssd_y_s, ssm_s, conv_s = ssd_sample(big_s, dt_s, state_ssm[0].reshape(bs, D_INNER, D_STATE), state_conv[0],
                                        convw, convb, dtb, alog, d_skip_x, norm_g, expand, gsum)
    x1_s = mix(attn_s.reshape(bs, 1024).astype(BF16), ssd_y_s, big_s, xs_, wa, wb, wm, g_post_mix[0])
    y_s = _post_mixer(x1_s, cache_mem_k[0].reshape(bs, n_mem, D_MODEL), cache_mem_v[0].reshape(bs, n_mem, D_MODEL),
                      w, bs, 1)

    return (
        y_p.reshape(bp, t, D_MODEL),
        y_s.reshape(bs, 1, D_MODEL),
        k_p.reshape(1, bp, t, N_DIFF_KV, 2, HEAD_DIM),
        v_p.reshape(1, bp, t, N_DIFF_KV, 2 * HEAD_DIM),
        k_s.reshape(1, bs, 1, N_DIFF_KV, 2, HEAD_DIM),
        v_s.reshape(1, bs, 1, N_DIFF_KV, 2 * HEAD_DIM),
        ssm_p.reshape(1, bp, N_SSM_HEADS, SSM_HEAD_DIM, D_STATE),
        conv_p.reshape(1, bp, CONV_WIDTH - 1, CONV_DIM),
        ssm_s.reshape(1, bs, N_SSM_HEADS, SSM_HEAD_DIM, D_STATE),
        conv_s.reshape(1, bs, CONV_WIDTH - 1, CONV_DIM),
        mk_p.reshape(1, bp, n_mem, N_MEM_HEADS, MEM_HEAD_DIM),
        mv_p.reshape(1, bp, n_mem, N_MEM_HEADS, MEM_HEAD_DIM),
    )
```

```python
import functools
import math

import jax
import jax.numpy as jnp
import numpy as np
from jax import lax
from jax.experimental import pallas as pl
from jax.experimental.pallas import tpu as pltpu

F32 = jnp.float32
BF16 = jnp.bfloat16

D_MODEL = 1024
HEAD_DIM = 64
N_DIFF_HEADS = 8
N_DIFF_KV = 4
GQA_REP = 2
D_INNER = 2048
SSM_HEAD_DIM = 64
N_SSM_HEADS = 32
N_SSM_GROUPS = 4
D_STATE = 128
CONV_WIDTH = 4
CONV_DIM = 3072
SSD_CHUNK = 128
N_MEM_HEADS = 4
MEM_HEAD_DIM = 256
D_FF = 4096
EPS = 1e-6
PAGE_SIZE = 128

BIG_XBC, BIG_Q, BIG_Z, BIG_GA, BIG_GB = 0, 3072, 4096, 6144, 7168
BIG_WIDTH = 8192
DT_PAD = 128

VMEM_LIMIT_BYTES = 48 * 1024 * 1024

NT_DIMS = (((1,), (1,)), ((), ()))
LOG2E = 1.4426950408889634


def _cparams(n_axes):
    return pltpu.CompilerParams(dimension_semantics=("arbitrary",) * n_axes,
                                vmem_limit_bytes=VMEM_LIMIT_BYTES)


def _rms(x, g):
    ms = jnp.mean(x * x, axis=-1, keepdims=True)
    return x * lax.rsqrt(ms + EPS) * g


def _sigmoid(x):
    return 1.0 / (1.0 + jnp.exp(-x))


def _silu(x):
    return x * _sigmoid(x)


def _softplus(x):
    return jnp.maximum(x, 0.0) + jnp.log1p(jnp.exp(-jnp.abs(x)))


def _split3(x):
    hi = x.astype(BF16)
    r1 = x - hi.astype(F32)
    mid = r1.astype(BF16)
    lo = (r1 - mid.astype(F32)).astype(BF16)
    return hi, mid, lo


def _dot3(x, sel):
    hi, mid, lo = _split3(x)
    f = lambda a: jnp.dot(a, sel, preferred_element_type=F32)
    return f(hi) + f(mid) + f(lo)


def _sel_dot3(sel, x):
    hi, mid, lo = _split3(x)
    f = lambda a: jnp.dot(sel, a, preferred_element_type=F32)
    return f(hi) + f(mid) + f(lo)


def _norm_proj_kernel(*refs, has_big, n_small, n_trans):
    x_ref, g_ref = refs[0], refs[1]
    idx = 2
    wb_ref = None
    if has_big:
        wb_ref = refs[idx]
        idx += 1
    ws = refs[idx:idx + n_small]
    idx += n_small
    wts = refs[idx:idx + n_trans]
    idx += n_trans
    ob_ref = None
    if has_big:
        ob_ref = refs[idx]
        idx += 1
    outs = refs[idx:idx + n_small]
    idx += n_small
    outs_t = refs[idx:idx + n_trans]
    idx += n_trans
    h_ref = refs[idx]

    @pl.when(pl.program_id(1) == 0)
    def _():
        h = _rms(x_ref[...], g_ref[...]).astype(BF16)
        h_ref[...] = h
        for w, o in zip(ws, outs):
            o[...] = jnp.dot(h, w[...], preferred_element_type=F32)
        for wt, o in zip(wts, outs_t):
            o[0] = lax.dot_general(wt[...], h, NT_DIMS, preferred_element_type=F32)

    if has_big:
        ob_ref[...] = jnp.dot(h_ref[...], wb_ref[...], preferred_element_type=F32).astype(BF16)


def norm_proj(x, g, w_big, w_smalls, w_trans=(), *, rows_per_batch=None, tm=1024, tn=512):
    m, d = x.shape
    t = rows_per_batch if rows_per_batch is not None else m
    tm = min(tm, t)
    nt = t // tm
    has_big = w_big is not None
    nb = w_big.shape[1] if has_big else 0
    tn = min(tn, nb) if has_big else 0
    nj = nb // tn if has_big else 1
    in_specs = [pl.BlockSpec((tm, d), lambda i, j: (i, 0)), pl.BlockSpec((1, d), lambda i, j: (0, 0))]
    args = [x, g.reshape(1, d)]
    out_shape, out_specs = [], []
    if has_big:
        in_specs.append(pl.BlockSpec((d, tn), lambda i, j: (0, j)))
        args.append(w_big)
        out_shape.append(jax.ShapeDtypeStruct((m, nb), BF16))
        out_specs.append(pl.BlockSpec((tm, tn), lambda i, j: (i, j)))
    for w in list(w_smalls) + list(w_trans):
        in_specs.append(pl.BlockSpec(w.shape, lambda i, j: (0, 0)))
        args.append(w)
    for w in w_smalls:
        out_shape.append(jax.ShapeDtypeStruct((m, w.shape[1]), F32))
        out_specs.append(pl.BlockSpec((tm, w.shape[1]), lambda i, j: (i, 0)))
    for w in w_trans:
        out_shape.append(jax.ShapeDtypeStruct((m // t, w.shape[0], t), F32))
        out_specs.append(pl.BlockSpec((1, w.shape[0], tm), lambda i, j: (i // nt, 0, i % nt)))
    outs = pl.pallas_call(
        functools.partial(_norm_proj_kernel, has_big=has_big, n_small=len(w_smalls), n_trans=len(w_trans)),
        grid=(m // tm, nj),
        in_specs=in_specs,
        out_specs=out_specs,
        out_shape=out_shape,
        scratch_shapes=[pltpu.VMEM((tm, d), BF16)],
        compiler_params=_cparams(2),
        name="norm_proj",
    )(*args)
    outs = list(outs)
    big = outs.pop(0) if has_big else None
    return big, outs[:len(w_smalls)], outs[len(w_smalls):]


def _lambda_value(lamv_ref, lambda_init):
    lv = lamv_ref[...]
    s1 = jnp.sum(lv[0:1] * lv[1:2], axis=-1, keepdims=True)
    s2 = jnp.sum(lv[2:3] * lv[3:4], axis=-1, keepdims=True)
    return jnp.exp(s1) - jnp.exp(s2) + lambda_init


N_BIAS_ROWS = 6


def _diff_attn_kernel(sl_ref, q_ref, kt_ref, v_ref, lamv_ref, subg_ref, o_ref, qa_ref, m_ref, l_ref, acc_ref, *,
                      tq, tk, lambda_init):
    kv = pl.program_id(1)
    qi = pl.program_id(2)
    ki = pl.program_id(3)
    nchunk = tk // 128

    @pl.when(ki == 0)
    def _():
        m_ref[...] = jnp.full(m_ref.shape, -jnp.inf, F32)
        l_ref[...] = jnp.zeros(l_ref.shape, F32)
        acc_ref[...] = jnp.zeros(acc_ref.shape, F32)
        lane = lax.broadcasted_iota(jnp.int32, (tq, HEAD_DIM), 1)
        for r in range(GQA_REP):
            hi = sl_ref[0, kv * GQA_REP + r]
            lo = sl_ref[1, kv * GQA_REP + r]
            sblk = jnp.where(lane < 3, hi, jnp.where(lane < N_BIAS_ROWS, lo, 0.0)).astype(BF16)
            for i in range(2):
                c = r * 2 + i
                qa_ref[i, r * tq:(r + 1) * tq, 0:HEAD_DIM] = q_ref[0, :, c * HEAD_DIM:(c + 1) * HEAD_DIM]
                qa_ref[i, r * tq:(r + 1) * tq, HEAD_DIM:2 * HEAD_DIM] = sblk

    def update(masked):
        kt = kt_ref[0].astype(BF16)
        v = v_ref[0].astype(BF16)
        rowid = lax.broadcasted_iota(jnp.int32, (16, tk), 0)
        colid = lax.broadcasted_iota(jnp.int32, (16, tk), 1)
        off = (ki * tk - qi * tq).astype(F32)
        tile_off = (colid & ~127).astype(F32)
        lane_off = (colid & 127).astype(F32)
        bias_rows = jnp.zeros((16, tk), F32)
        for base in (0, 3):
            for n, val in enumerate((off, tile_off, lane_off)):
                bias_rows = jnp.where(rowid == base + n, val, bias_rows)
        bias_blk = jnp.concatenate([bias_rows.astype(BF16), jnp.zeros((HEAD_DIM - 16, tk), BF16)], axis=0)
        if masked:
            row = lax.broadcasted_iota(jnp.int32, (GQA_REP * tq, 128), 0) & (tq - 1)
            col = lax.broadcasted_iota(jnp.int32, (GQA_REP * tq, 128), 1)
        for i in range(2):
            kaug = jnp.concatenate([kt[i * HEAD_DIM:(i + 1) * HEAD_DIM, :], bias_blk], axis=0)
            s = jnp.dot(qa_ref[i], kaug, preferred_element_type=F32)
            sc = [s[:, j * 128:(j + 1) * 128] for j in range(nchunk)]
            if masked:
                sc = [jnp.where(col + j * 128 <= row, sc[j], -jnp.inf) for j in range(nchunk)]
            mc = sc[0]
            for j in range(1, nchunk):
                mc = jnp.maximum(mc, sc[j])
            m_prev = m_ref[i]
            m_new = jnp.maximum(m_prev, jnp.max(mc, axis=-1, keepdims=True))
            alpha = jnp.exp2(m_prev - m_new)
            ps = [jnp.exp2(sc[j] - m_new) for j in range(nchunk)]
            lsum = ps[0]
            for j in range(1, nchunk):
                lsum = lsum + ps[j]
            l_ref[i] = alpha * l_ref[i] + lsum
            p = jnp.concatenate([pj.astype(BF16) for pj in ps], axis=-1)
            acc_ref[i] = alpha * acc_ref[i] + jnp.dot(p, v, preferred_element_type=F32)
            m_ref[i] = m_new

    @pl.when(ki < qi)
    def _():
        update(False)

    @pl.when(ki == qi)
    def _():
        update(True)
        lam = _lambda_value(lamv_ref, lambda_init)
        for r in range(GQA_REP):
            rows = slice(r * tq, (r + 1) * tq)
            o0 = acc_ref[0, rows] / jnp.sum(l_ref[0, rows], axis=-1, keepdims=True)
            o1 = acc_ref[1, rows] / jnp.sum(l_ref[1, rows], axis=-1, keepdims=True)
            d = _rms(o0 - lam * o1, subg_ref[...]) * (1.0 - lambda_init)
            o_ref[0, :, r * 128:(r + 1) * 128] = d.astype(o_ref.dtype)


def diff_attn_prompt(big, kt, v, slopes_hl, lamv, subg, lambda_init, *, tile=512):
    b, t, _ = big.shape
    tq = tk = min(tile, t)
    nq = t // tq
    q_blk0 = BIG_Q // 256
    grid_spec = pltpu.PrefetchScalarGridSpec(
        num_scalar_prefetch=1,
        grid=(b, N_DIFF_KV, nq, nq),
        in_specs=[
            pl.BlockSpec((1, tq, 256), lambda bi, kv, qi, ki, sl: (bi, qi, q_blk0 + kv)),
            pl.BlockSpec((1, 128, tk), lambda bi, kv, qi, ki, sl: (bi, kv, jnp.minimum(ki, qi))),
            pl.BlockSpec((1, tk, 128), lambda bi, kv, qi, ki, sl: (bi, jnp.minimum(ki, qi), kv)),
            pl.BlockSpec((4, HEAD_DIM), lambda bi, kv, qi, ki, sl: (0, 0)),
            pl.BlockSpec((1, 128), lambda bi, kv, qi, ki, sl: (0, 0)),
        ],
        out_specs=pl.BlockSpec((1, tq, 256), lambda bi, kv, qi, ki, sl: (bi, qi, kv)),
        scratch_shapes=[pltpu.VMEM((2, GQA_REP * tq, 128), BF16), pltpu.VMEM((2, GQA_REP * tq, 128), F32),
                        pltpu.VMEM((2, GQA_REP * tq, 128), F32), pltpu.VMEM((2, GQA_REP * tq, 128), F32)],
    )
    return pl.pallas_call(
        functools.partial(_diff_attn_kernel, tq=tq, tk=tk, lambda_init=lambda_init),
        grid_spec=grid_spec,
        out_shape=jax.ShapeDtypeStruct((b, t, 1024), BF16),
        compiler_params=_cparams(4),
        name="diff_attn_prompt",
    )(slopes_hl, big, kt, v, lamv, subg)


def _diff_attn_paged_kernel(pt_ref, qbd_ref, knew_ref, vnew_ref, slope_ref, lamv_ref, subg_ref, *refs,
                            pages_per_step, past_len, lambda_init):
    del pt_ref
    k_refs = refs[:pages_per_step]
    v_refs = refs[pages_per_step:2 * pages_per_step]
    o_ref, m_ref, l_ref, acc_ref = refs[2 * pages_per_step:]
    j = pl.program_id(1)
    qbd = qbd_ref[0]
    row_kv = lax.broadcasted_iota(jnp.int32, (16, 128), 0) // 4

    @pl.when(j == 0)
    def _():
        s_self = jnp.sum(qbd.astype(F32) * knew_ref[0], axis=-1, keepdims=True)
        m_ref[...] = s_self
        l_ref[...] = jnp.ones(l_ref.shape, F32)
        vn = vnew_ref[0]
        acc = jnp.zeros((16, 128), F32)
        for kv in range(N_DIFF_KV):
            acc = jnp.where(row_kv == kv, vn[:, kv * 128:(kv + 1) * 128], acc)
        acc_ref[...] = acc

    tk = pages_per_step * PAGE_SIZE
    kt_all = jnp.concatenate([kr[...].astype(BF16) for kr in k_refs], axis=-1)
    s = jnp.dot(qbd, kt_all, preferred_element_type=F32)
    kpos = (j * tk + lax.broadcasted_iota(jnp.int32, (1, tk), 1) - past_len).astype(F32)
    s = s + slope_ref[...] * kpos
    m_prev = m_ref[...]
    m_new = jnp.maximum(m_prev, jnp.max(s, axis=-1, keepdims=True))
    alpha = jnp.exp2(m_prev - m_new)
    p = jnp.exp2(s - m_new)
    l_ref[...] = alpha * l_ref[...] + jnp.sum(p, axis=-1, keepdims=True)
    pb = p.astype(BF16)
    zero = jnp.zeros((16, PAGE_SIZE), BF16)
    p_parts, v_parts = [], []
    for n in range(pages_per_step):
        pn = pb[:, n * PAGE_SIZE:(n + 1) * PAGE_SIZE]
        for kv in range(N_DIFF_KV):
            p_parts.append(jnp.where(row_kv == kv, pn, zero))
            v_parts.append(v_refs[n][pl.ds(kv, PAGE_SIZE, stride=N_DIFF_KV), :].astype(BF16))
    pv = jnp.dot(jnp.concatenate(p_parts, axis=-1), jnp.concatenate(v_parts, axis=0), preferred_element_type=F32)
    acc_ref[...] = alpha * acc_ref[...] + pv
    m_ref[...] = m_new

    @pl.when(j == pl.num_programs(1) - 1)
    def _():
        lam = _lambda_value(lamv_ref, lambda_init)
        o = acc_ref[...] / l_ref[...]
        for c in range(N_DIFF_HEADS):
            d = o[2 * c:2 * c + 1] - lam * o[2 * c + 1:2 * c + 2]
            d = _rms(d, subg_ref[...]) * (1.0 - lambda_init)
            o_ref[0, :, c * 128:(c + 1) * 128] = d.astype(o_ref.dtype)


def diff_attn_paged(qbd, k_new, v_new, cache_kt, cache_v, page_table, slope_rows, lamv, subg, lambda_init,
                    *, pages_per_step=8):
    bs, n_pages = page_table.shape
    pps = min(pages_per_step, n_pages)
    nsteps = n_pages // pps

    def page_spec(n):
        return pl.BlockSpec((None, 512, PAGE_SIZE), lambda b, j, pt: (pt[b, j * pps + n], 0, 0))

    grid_spec = pltpu.PrefetchScalarGridSpec(
        num_scalar_prefetch=1,
        grid=(bs, nsteps),
        in_specs=[
            pl.BlockSpec((1, 16, 512), lambda b, j, pt: (b, 0, 0)),
            pl.BlockSpec((1, 1, 512), lambda b, j, pt: (b, 0, 0)),
            pl.BlockSpec((1, 1, 512), lambda b, j, pt: (b, 0, 0)),
            pl.BlockSpec((16, 1), lambda b, j, pt: (0, 0)),
            pl.BlockSpec((4, HEAD_DIM), lambda b, j, pt: (0, 0)),
            pl.BlockSpec((1, 128), lambda b, j, pt: (0, 0)),
        ] + [page_spec(n) for n in range(pps)] + [page_spec(n) for n in range(pps)],
        out_specs=pl.BlockSpec((1, 1, 1024), lambda b, j, pt: (b, 0, 0)),
        scratch_shapes=[pltpu.VMEM((16, 1), F32), pltpu.VMEM((16, 1), F32), pltpu.VMEM((16, 128), F32)],
    )
    return pl.pallas_call(
        functools.partial(_diff_attn_paged_kernel, pages_per_step=pps, past_len=n_pages * PAGE_SIZE,
                          lambda_init=lambda_init),
        grid_spec=grid_spec,
        out_shape=jax.ShapeDtypeStruct((bs, 1, 1024), F32),
        compiler_params=_cparams(2),
        name="diff_attn_paged",
    )(page_table, qbd, k_new, v_new, slope_rows, lamv, subg, *([cache_kt] * pps), *([cache_v] * pps))


def _ssd_prompt_kernel(xbc_ref, z_ref, dt_ref, convw_ref, convb_ref, dtb_ref, alog_ref, dskip_ref, ng_ref,
                       expand_ref, tril_ref, y_ref, st_out_ref, conv_out_ref, xbuf_ref, st_ref):
    c = pl.program_id(1)
    L = SSD_CHUNK
    GW = D_INNER // N_SSM_GROUPS

    @pl.when(c == 0)
    def _():
        xbuf_ref[0:8, :] = jnp.zeros((8, CONV_DIM), F32)
        st_ref[...] = jnp.zeros(st_ref.shape, F32)

    xc = xbc_ref[0].astype(F32)
    xbuf_ref[8:8 + L, :] = xc
    w = convw_ref[...]
    conv = (convb_ref[...] + xbuf_ref[5:5 + L, :] * w[0:1] + xbuf_ref[6:6 + L, :] * w[1:2]
            + xbuf_ref[7:7 + L, :] * w[2:3] + xc * w[3:4])
    tail = xc[L - 3:L, :]
    xbuf_ref[5:8, :] = tail

    @pl.when(c == pl.num_programs(1) - 1)
    def _():
        conv_out_ref[0] = tail

    act = _silu(conv)
    xs = act[:, :D_INNER]
    xs_b = xs.astype(BF16)

    lane = lax.broadcasted_iota(jnp.int32, (L, DT_PAD), 1)
    dt = jnp.where(lane < N_SSM_HEADS, _softplus(dt_ref[0] + dtb_ref[...]), 0.0)
    d_a = dt * (-jnp.exp(alog_ref[...]))
    tril = tril_ref[...]
    a_cs = _sel_dot3(tril, d_a)
    a_cs_t = a_cs.T
    dt_t = dt.T
    a_end = a_cs[L - 1:L, :]
    expand = expand_ref[...]
    ea_x = _dot3(jnp.exp(a_cs), expand)
    w_x = _dot3(jnp.exp(a_end - a_cs) * dt, expand)
    xw_b = (xs * w_x).astype(BF16)

    row = lax.broadcasted_iota(jnp.int32, (L, L), 0)
    col = lax.broadcasted_iota(jnp.int32, (L, L), 1)
    causal = row >= col
    half = lax.broadcasted_iota(jnp.int32, (L, 128), 1) < SSM_HEAD_DIM

    y_parts = []
    for g in range(N_SSM_GROUPS):
        bg = act[:, D_INNER + g * D_STATE:D_INNER + (g + 1) * D_STATE]
        cg = act[:, D_INNER + N_SSM_GROUPS * D_STATE + g * D_STATE:D_INNER + N_SSM_GROUPS * D_STATE + (g + 1) * D_STATE]
        bg_b = bg.astype(BF16)
        cg_b = cg.astype(BF16)
        cb = lax.dot_general(cg_b, bg_b, NT_DIMS, preferred_element_type=F32)
        st_g = st_ref[:, g * GW:(g + 1) * GW]
        y_off = jnp.dot(cg_b, st_g.astype(BF16), preferred_element_type=F32) * ea_x[:, g * GW:(g + 1) * GW]
        y_diag_parts = []
        for pair in range(GW // 128):
            h0 = g * (GW // SSM_HEAD_DIM) + 2 * pair
            x_pair = xs_b[:, g * GW + pair * 128:g * GW + (pair + 1) * 128]
            ys = []
            for h in (h0, h0 + 1):
                diff = a_cs[:, h:h + 1] - a_cs_t[h:h + 1, :]
                decay = jnp.exp(jnp.where(causal, diff, -jnp.inf))
                mh = (cb * decay * dt_t[h:h + 1, :]).astype(BF16)
                ys.append(jnp.dot(mh, x_pair, preferred_element_type=F32))
            y_diag_parts.append(jnp.where(half, ys[0], ys[1]))
        y_parts.append(jnp.concatenate(y_diag_parts, axis=-1) + y_off)
        new_g = jnp.dot(bg.T.astype(BF16), xw_b[:, g * GW:(g + 1) * GW], preferred_element_type=F32)
        st_ref[:, g * GW:(g + 1) * GW] = st_g * ea_x[L - 1:L, g * GW:(g + 1) * GW] + new_g

    y = jnp.concatenate(y_parts, axis=-1) + dskip_ref[...] * xs
    y = y * _silu(z_ref[0].astype(F32))
    ng = ng_ref[...]
    for g in range(N_SSM_GROUPS):
        y_ref[0, :, g * GW:(g + 1) * GW] = _rms(y[:, g * GW:(g + 1) * GW], ng[:, g * GW:(g + 1) * GW]).astype(y_ref.dtype)

    @pl.when(c == pl.num_programs(1) - 1)
    def _():
        st_out_ref[0] = st_ref[...].T


def ssd_prompt(big, dt_raw, conv_w, conv_b, dt_bias, a_log, d_skip_x, norm_g, expand, tril):
    b, t, _ = big.shape
    nc = t // SSD_CHUNK
    const = lambda shape: pl.BlockSpec(shape, lambda bi, ci: (0,) * len(shape))
    return pl.pallas_call(
        _ssd_prompt_kernel,
        grid=(b, nc),
        in_specs=[
            pl.BlockSpec((1, SSD_CHUNK, CONV_DIM), lambda bi, ci: (bi, ci, BIG_XBC // CONV_DIM)),
            pl.BlockSpec((1, SSD_CHUNK, D_INNER), lambda bi, ci: (bi, ci, BIG_Z // D_INNER)),
            pl.BlockSpec((1, SSD_CHUNK, DT_PAD), lambda bi, ci: (bi, ci, 0)),
            const((CONV_WIDTH, CONV_DIM)), const((1, CONV_DIM)), const((1, DT_PAD)), const((1, DT_PAD)),
            const((1, D_INNER)), const((1, D_INNER)), const((DT_PAD, D_INNER)), const((SSD_CHUNK, SSD_CHUNK)),
        ],
        out_specs=[
            pl.BlockSpec((1, SSD_CHUNK, D_INNER), lambda bi, ci: (bi, ci, 0)),
            pl.BlockSpec((1, D_INNER, D_STATE), lambda bi, ci: (bi, 0, 0)),
            pl.BlockSpec((1, CONV_WIDTH - 1, CONV_DIM), lambda bi, ci: (bi, 0, 0)),
        ],
        out_shape=[
            jax.ShapeDtypeStruct((b, t, D_INNER), BF16),
            jax.ShapeDtypeStruct((b, D_INNER, D_STATE), F32),
            jax.ShapeDtypeStruct((b, CONV_WIDTH - 1, CONV_DIM), F32),
        ],
        scratch_shapes=[pltpu.VMEM((8 + SSD_CHUNK, CONV_DIM), F32), pltpu.VMEM((D_STATE, D_INNER), F32)],
        compiler_params=_cparams(2),
        name="ssd_prompt",
    )(big, big, dt_raw, conv_w, conv_b, dt_bias, a_log, d_skip_x, norm_g, expand, tril)


def _ssd_step_pre_kernel(xbc_ref, c0_ref, c1_ref, c2_ref, dt_ref, convw_ref, convb_ref, dtb_ref, alog_ref, dskip_ref,
                         expand_ref, gsum_ref, dtx_ref, cd_ref, ypre_ref, b_ref, c_ref):
    xnew = xbc_ref[...].astype(F32)
    w = convw_ref[...]
    conv = (convb_ref[...] + c0_ref[...] * w[0:1] + c1_ref[...] * w[1:2] + c2_ref[...] * w[2:3] + xnew * w[3:4])
    act = _silu(conv)
    xs = act[:, :D_INNER]
    bm = act[:, D_INNER:D_INNER + N_SSM_GROUPS * D_STATE]
    cm = act[:, D_INNER + N_SSM_GROUPS * D_STATE:]
    lane = lax.broadcasted_iota(jnp.int32, dt_ref.shape, 1)
    dt = jnp.where(lane < N_SSM_HEADS, _softplus(dt_ref[...] + dtb_ref[...]), 0.0)
    d_a = dt * (-jnp.exp(alog_ref[...]))
    expand = expand_ref[...]
    dt_x = _dot3(dt, expand)
    cd_x = _dot3(jnp.exp(d_a), expand)
    dtx = dt_x * xs
    cb_x = _dot3(cm.astype(BF16).astype(F32) * bm.astype(BF16).astype(F32), gsum_ref[...])
    dtx_ref[...] = dtx
    cd_ref[...] = cd_x
    ypre_ref[...] = cb_x * dtx + dskip_ref[...] * xs
    b_ref[...] = bm
    c_ref[...] = cm


def _ssd_step_state_kernel(st_ref, dtxt_ref, cdt_ref, cd_ref, ypre_ref, z_ref, b_ref, c_ref, ng_ref, stout_ref, y_ref):
    dtx_t = dtxt_ref[0]
    cd_t = cdt_ref[0]
    rows_per_group = D_INNER // N_SSM_GROUPS
    yoff_rows = []
    for j in range(D_INNER // 128):
        g = (j * 128) // rows_per_group
        h0 = st_ref[0, j * 128:(j + 1) * 128, :]
        bg = b_ref[0, g:g + 1, :]
        cg = c_ref[0, g:g + 1, :]
        stout_ref[0, j * 128:(j + 1) * 128, :] = h0 * cd_t[:, j:j + 1] + dtx_t[:, j:j + 1] * bg
        cg8 = jnp.broadcast_to(cg, (8, D_STATE)).astype(BF16)
        yo = lax.dot_general(cg8, h0.astype(BF16), NT_DIMS, preferred_element_type=F32)
        yoff_rows.append(yo[0:1])
    y = ypre_ref[0] + jnp.concatenate(yoff_rows, axis=0) * cd_ref[0]
    y = y * _silu(z_ref[0].astype(F32))
    ng = ng_ref[...]
    rpg = 16 // N_SSM_GROUPS
    for g in range(N_SSM_GROUPS):
        yg = y[g * rpg:(g + 1) * rpg]
        ms = jnp.sum(jnp.sum(yg * yg, axis=-1, keepdims=True), axis=0, keepdims=True) / (rpg * 128)
        y_ref[0, g * rpg:(g + 1) * rpg, :] = (yg * lax.rsqrt(ms + EPS) * ng[g * rpg:(g + 1) * rpg]).astype(y_ref.dtype)


def ssd_sample(big, dt_raw, state, conv0, conv_w, conv_b, dt_bias, a_log, d_skip_x, norm_g, expand, gsum):
    bs = big.shape[0]
    full = lambda shape: pl.BlockSpec(shape, lambda i: (0,) * len(shape))
    dtx, cd, ypre, bm, cm = pl.pallas_call(
        _ssd_step_pre_kernel,
        grid=(1,),
        in_specs=[
            pl.BlockSpec((bs, CONV_DIM), lambda i: (0, BIG_XBC // CONV_DIM)),
            full((bs, CONV_DIM)), full((bs, CONV_DIM)), full((bs, CONV_DIM)),
            full((bs, DT_PAD)), full((CONV_WIDTH, CONV_DIM)),
            full((1, CONV_DIM)), full((1, DT_PAD)), full((1, DT_PAD)), full((1, D_INNER)),
            full((DT_PAD, D_INNER)), full((N_SSM_GROUPS * D_STATE, D_INNER)),
        ],
        out_specs=[full((bs, D_INNER)), full((bs, D_INNER)),
                   full((bs, D_INNER)), full((bs, N_SSM_GROUPS * D_STATE)), full((bs, N_SSM_GROUPS * D_STATE))],
        out_shape=[
            jax.ShapeDtypeStruct((bs, D_INNER), F32), jax.ShapeDtypeStruct((bs, D_INNER), F32),
            jax.ShapeDtypeStruct((bs, D_INNER), F32),
            jax.ShapeDtypeStruct((bs, N_SSM_GROUPS * D_STATE), F32),
            jax.ShapeDtypeStruct((bs, N_SSM_GROUPS * D_STATE), F32),
        ],
        compiler_params=_cparams(1),
        name="ssd_step_pre",
    )(big, conv0[:, 0], conv0[:, 1], conv0[:, 2], dt_raw, conv_w, conv_b, dt_bias, a_log, d_skip_x, expand, gsum)
    newconv = jnp.concatenate([conv0[:, 1:], big[:, BIG_XBC:BIG_XBC + CONV_DIM].astype(F32)[:, None]], axis=1)

    z3 = big[:, BIG_Z:BIG_Z + D_INNER].reshape(bs, 16, 128)
    r3 = lambda a: a.reshape(bs, 16, 128)
    t3 = lambda a: jnp.swapaxes(r3(a), 1, 2)
    g3 = lambda a: a.reshape(bs, N_SSM_GROUPS, D_STATE)
    per_seq = lambda shape: pl.BlockSpec((1,) + shape, lambda i: (i, 0, 0))
    st_new, y = pl.pallas_call(
        _ssd_step_state_kernel,
        grid=(bs,),
        in_specs=[per_seq((D_INNER, D_STATE)), per_seq((128, 16)), per_seq((128, 16)), per_seq((16, 128)),
                  per_seq((16, 128)), per_seq((16, 128)), per_seq((N_SSM_GROUPS, D_STATE)),
                  per_seq((N_SSM_GROUPS, D_STATE)), pl.BlockSpec((16, 128), lambda i: (0, 0))],
        out_specs=[per_seq((D_INNER, D_STATE)), per_seq((16, 128))],
        out_shape=[jax.ShapeDtypeStruct((bs, D_INNER, D_STATE), F32), jax.ShapeDtypeStruct((bs, 16, 128), BF16)],
        compiler_params=_cparams(1),
        name="ssd_step_state",
    )(state, t3(dtx), t3(cd), r3(cd), r3(ypre), z3, g3(bm), g3(cm), norm_g.reshape(16, 128))
    return y.reshape(bs, D_INNER), st_new, newconv


def _mix_kernel(attn_ref, ssd_ref, ga_ref, gb_ref, x_ref, wa_ref, wb_ref, wm_ref, g_ref, o_ref):
    ya = jnp.dot(attn_ref[...], wa_ref[...], preferred_element_type=F32)
    yb = jnp.dot(ssd_ref[...], wb_ref[...], preferred_element_type=F32)
    merged = _sigmoid(ga_ref[...].astype(F32)) * ya + _sigmoid(gb_ref[...].astype(F32)) * yb
    mo = jnp.dot(merged.astype(BF16), wm_ref[...], preferred_element_type=F32)
    o_ref[...] = x_ref[...] + _rms(mo, g_ref[...])


def mix(attn, ssd_y, big, x, wa, wb, wm, g, *, tm=512):
    m = x.shape[0]
    tm = min(tm, m)
    full = lambda a: pl.BlockSpec(a.shape, lambda i: (0, 0))
    return pl.pallas_call(
        _mix_kernel,
        grid=(m // tm,),
        in_specs=[
            pl.BlockSpec((tm, 1024), lambda i: (i, 0)),
            pl.BlockSpec((tm, D_INNER), lambda i: (i, 0)),
            pl.BlockSpec((tm, 1024), lambda i: (i, BIG_GA // 1024)),
            pl.BlockSpec((tm, 1024), lambda i: (i, BIG_GB // 1024)),
            pl.BlockSpec((tm, D_MODEL), lambda i: (i, 0)),
            full(wa), full(wb), full(wm), pl.BlockSpec((1, D_MODEL), lambda i: (0, 0)),
        ],
        out_specs=pl.BlockSpec((tm, D_MODEL), lambda i: (i, 0)),
        out_shape=jax.ShapeDtypeStruct((m, D_MODEL), F32),
        compiler_params=_cparams(1),
        name="mix",
    )(attn, ssd_y, big, big, x, wa, wb, wm, g.reshape(1, D_MODEL))


def _cross_attn_kernel(q_ref, mk_ref, mv_ref, o_ref, *, tq):
    q = q_ref[0]
    rows = max(tq, 8)
    if tq < 8:
        q = jnp.broadcast_to(q.astype(F32), (rows, D_MODEL)).astype(BF16)
    for h in range(N_MEM_HEADS):
        sl = slice(h * MEM_HEAD_DIM, (h + 1) * MEM_HEAD_DIM)
        kh = mk_ref[0, :, sl].astype(BF16)
        vh = mv_ref[0, :, sl].astype(BF16)
        s = lax.dot_general(q[:, sl], kh, NT_DIMS, preferred_element_type=F32) * (MEM_HEAD_DIM ** -0.5)
        p = jnp.exp(s - jnp.max(s, axis=-1, keepdims=True))
        a = p / jnp.sum(p, axis=-1, keepdims=True)
        oh = jnp.dot(a.astype(BF16), vh, preferred_element_type=F32)
        o_ref[0, :, sl] = oh[0:tq].astype(o_ref.dtype)


def cross_attn(q, mk, mv, *, tq=512):
    b, t, _ = q.shape
    tq = min(tq, t)
    n_mem = mk.shape[1]
    return pl.pallas_call(
        functools.partial(_cross_attn_kernel, tq=tq),
        grid=(b, t // tq),
        in_specs=[
            pl.BlockSpec((1, tq, D_MODEL), lambda bi, ti: (bi, ti, 0)),
            pl.BlockSpec((1, n_mem, D_MODEL), lambda bi, ti: (bi, 0, 0)),
            pl.BlockSpec((1, n_mem, D_MODEL), lambda bi, ti: (bi, 0, 0)),
        ],
        out_specs=pl.BlockSpec((1, tq, D_MODEL), lambda bi, ti: (bi, ti, 0)),
        out_shape=jax.ShapeDtypeStruct((b, t, D_MODEL), BF16),
        compiler_params=_cparams(2),
        name="cross_attn",
    )(q, mk, mv)


def _proj_norm_res_kernel(a_ref, w_ref, g_ref, res_ref, o_ref):
    y = jnp.dot(a_ref[...], w_ref[...], preferred_element_type=F32)
    o_ref[...] = res_ref[...] + _rms(y, g_ref[...])


def proj_norm_res(a, w, g, res, *, tm=1024):
    m, k = a.shape
    tm = min(tm, m)
    return pl.pallas_call(
        _proj_norm_res_kernel,
        grid=(m // tm,),
        in_specs=[pl.BlockSpec((tm, k), lambda i: (i, 0)), pl.BlockSpec(w.shape, lambda i: (0, 0)),
                  pl.BlockSpec((1, D_MODEL), lambda i: (0, 0)), pl.BlockSpec((tm, D_MODEL), lambda i: (i, 0))],
        out_specs=pl.BlockSpec((tm, D_MODEL), lambda i: (i, 0)),
        out_shape=jax.ShapeDtypeStruct((m, D_MODEL), F32),
        compiler_params=_cparams(1),
        name="proj_norm_res",
    )(a, w, g.reshape(1, D_MODEL), res)


def _mlp_kernel(x_ref, gpre_ref, wup_ref, wdown_ref, gpost_ref, o_ref, h_ref, acc_ref):
    j = pl.program_id(1)

    @pl.when(j == 0)
    def _():
        h_ref[...] = _rms(x_ref[...], gpre_ref[...]).astype(BF16)
        acc_ref[...] = jnp.zeros(acc_ref.shape, F32)

    u = jnp.maximum(jnp.dot(h_ref[...], wup_ref[...], preferred_element_type=F32), 0.0)
    acc_ref[...] += jnp.dot((u * u).astype(BF16), wdown_ref[...], preferred_element_type=F32)

    @pl.when(j == pl.num_programs(1) - 1)
    def _():
        o_ref[...] = x_ref[...] + _rms(acc_ref[...], gpost_ref[...])


def mlp(x, g_pre, w_up, w_down, g_post, *, tm=1024, tf=512):
    m = x.shape[0]
    tm = min(tm, m)
    return pl.pallas_call(
        _mlp_kernel,
        grid=(m // tm, D_FF // tf),
        in_specs=[
            pl.BlockSpec((tm, D_MODEL), lambda i, j: (i, 0)),
            pl.BlockSpec((1, D_MODEL), lambda i, j: (0, 0)),
            pl.BlockSpec((D_MODEL, tf), lambda i, j: (0, j)),
            pl.BlockSpec((tf, D_MODEL), lambda i, j: (j, 0)),
            pl.BlockSpec((1, D_MODEL), lambda i, j: (0, 0)),
        ],
        out_specs=pl.BlockSpec((tm, D_MODEL), lambda i, j: (i, 0)),
        out_shape=jax.ShapeDtypeStruct((m, D_MODEL), F32),
        scratch_shapes=[pltpu.VMEM((tm, D_MODEL), BF16), pltpu.VMEM((tm, D_MODEL), F32)],
        compiler_params=_cparams(2),
        name="mlp",
    )(x, g_pre.reshape(1, D_MODEL), w_up, w_down, g_post.reshape(1, D_MODEL))


def _alibi_slopes():
    return np.asarray(2.0 ** (-8.0 * np.arange(1, N_DIFF_HEADS + 1) / N_DIFF_HEADS), np.float32)


def _post_mixer(x1, mk, mv, w, b, t):
    qc, _, _ = norm_proj(x1, w["g_pre_mem"], w["w_mem_q"], [], tn=1024)
    oc = cross_attn(qc.reshape(b, t, D_MODEL), mk, mv)
    x2 = proj_norm_res(oc.reshape(b * t, D_MODEL), w["w_mem_o"], w["g_post_mem"], x1)
    return mlp(x2, w["g_pre_mlp"], w["w_up"], w["w_down"], w["g_post_mlp"])


def kernel(x_prompt, x_sample, cache_k, cache_v, cache_mem_k, cache_mem_v, state_ssm, state_conv, page_table, mem_prompt, g_pre_mix, w_in, lambda_q1, lambda_k1, lambda_q2, lambda_k2, diff_norm_g, conv_w, conv_b, dt_bias, a_log, d_skip, ssm_norm_g, w_attn_branch, w_ssd_branch, w_mix_out, g_post_mix, g_pre_mem, mem_norm_g, w_mem_q, w_mem_k, w_mem_v, w_mem_o, g_post_mem, g_pre_mlp, w_up, w_down, g_post_mlp):
    depth = w_in.shape[0]
    assert depth == 1
    bp, t, _ = x_prompt.shape
    bs, n_pages = page_table.shape
    n_mem = mem_prompt.shape[1]
    n_pool = cache_k.shape[1]
    lambda_init = 0.8 - 0.6 * math.exp(-0.3 * 0)

    wi = w_in[0]
    offs = np.cumsum([0, 1024, 512, 512, D_INNER, CONV_DIM, N_SSM_HEADS, D_MODEL, D_MODEL])
    seg = lambda n: wi[:, offs[n]:offs[n + 1]]
    q_scale = (HEAD_DIM ** -0.5) * LOG2E
    w_big = jnp.concatenate([seg(4), seg(0) * q_scale, seg(3), seg(6), seg(7)], axis=1).astype(BF16)
    w_kt = seg(1).T.astype(BF16)
    w_v = seg(2).astype(BF16)
    w_dt = jnp.pad(seg(5), ((0, 0), (0, DT_PAD - N_SSM_HEADS))).astype(BF16)
    pad_heads = lambda a: jnp.pad(a.reshape(1, N_SSM_HEADS), ((0, 0), (0, DT_PAD - N_SSM_HEADS)))
    w = dict(
        g_pre_mem=g_pre_mem[0], w_mem_q=w_mem_q[0].astype(BF16), w_mem_o=w_mem_o[0].astype(BF16),
        g_post_mem=g_post_mem[0], g_pre_mlp=g_pre_mlp[0], w_up=w_up[0].astype(BF16),
        w_down=w_down[0].astype(BF16), g_post_mlp=g_post_mlp[0])
    wa = w_attn_branch[0].astype(BF16)
    wb = w_ssd_branch[0].astype(BF16)
    wm = w_mix_out[0].astype(BF16)
    lamv = jnp.stack([lambda_q1[0], lambda_k1[0], lambda_q2[0], lambda_k2[0]])
    subg = diff_norm_g[0].reshape(1, 2 * HEAD_DIM)
    sl2 = (_alibi_slopes().astype(np.float64) * LOG2E).astype(np.float32)
    sl_hi = sl2.astype(BF16).astype(np.float32)
    sl_lo = (sl2 - sl_hi).astype(BF16).astype(np.float32)
    slopes_hl = jnp.asarray(np.stack([sl_hi, sl_lo]))
    convw = conv_w[0]
    convb = conv_b[0].reshape(1, CONV_DIM)
    dtb = pad_heads(dt_bias[0])
    alog = pad_heads(a_log[0])
    d_skip_x = jnp.repeat(d_skip[0], SSM_HEAD_DIM).reshape(1, D_INNER)
    norm_g = ssm_norm_g[0].reshape(1, D_INNER)
    expand = jnp.asarray(np.kron(np.eye(DT_PAD, N_SSM_HEADS, dtype=np.float32),
                                 np.ones((1, SSM_HEAD_DIM), np.float32)), BF16)
    tril = jnp.asarray(np.tril(np.ones((SSD_CHUNK, SSD_CHUNK), np.float32)), BF16)
    gsum = jnp.asarray(np.kron(np.eye(N_SSM_GROUPS, dtype=np.float32),
                               np.ones((D_STATE, D_INNER // N_SSM_GROUPS), np.float32)), BF16)

    mp = bp * t
    xp = x_prompt.reshape(mp, D_MODEL)
    big_p, (v_p, dt_p), (kt_p,) = norm_proj(xp, g_pre_mix[0], w_big, [w_v, w_dt], [w_kt], rows_per_batch=t)
    big_p3 = big_p.reshape(bp, t, BIG_WIDTH)
    attn_p = diff_attn_prompt(big_p3, kt_p, v_p.reshape(bp, t, 512), slopes_hl, lamv, subg, lambda_init)
    ssd_y_p, ssm_p, conv_p = ssd_prompt(big_p3, dt_p.reshape(bp, t, DT_PAD), convw, convb, dtb, alog, d_skip_x,
                                        norm_g, expand, tril)
    x1_p = mix(attn_p.reshape(mp, 1024), ssd_y_p.reshape(mp, D_INNER), big_p, xp, wa, wb, wm, g_post_mix[0])
    _, (mk_p, mv_p), _ = norm_proj(mem_prompt.reshape(bp * n_mem, D_MODEL), mem_norm_g[0], None,
                                   [w_mem_k[0].astype(BF16), w_mem_v[0].astype(BF16)], tm=512)
    y_p = _post_mixer(x1_p, mk_p.reshape(bp, n_mem, D_MODEL), mv_p.reshape(bp, n_mem, D_MODEL), w, bp, t)

    xs_ = x_sample.reshape(bs, D_MODEL)
    big_s, (v_s, dt_s), (kt_s,) = norm_proj(xs_, g_pre_mix[0], w_big, [w_v, w_dt], [w_kt])
    k_s = kt_s[0].T
    q5 = big_s[:, BIG_Q:BIG_Q + 1024].reshape(bs, N_DIFF_KV, GQA_REP, 2, HEAD_DIM)
    eye_kv = jnp.eye(N_DIFF_KV, dtype=BF16)
    eye_h = jnp.eye(2, dtype=BF16)
    qbd = (q5[:, :, :, :, None, None, :] * eye_kv[None, :, None, None, :, None, None]
           * eye_h[None, None, None, :, None, :, None]).reshape(bs, 16, 512)
    slope_rows = jnp.asarray(np.repeat(sl2, 2).reshape(16, 1))
    cache_kt = jnp.transpose(cache_k, (0, 1, 3, 4, 5, 2)).reshape(depth * n_pool, 512, PAGE_SIZE)
    cache_v2 = cache_v.reshape(depth * n_pool, PAGE_SIZE * N_DIFF_KV, 2 * HEAD_DIM)
    attn_s = diff_attn_paged(qbd, k_s.reshape(bs, 1, 512), v_s.reshape(bs, 1, 512), cache_kt, cache_v2, page_table,
                             slope_rows, lamv, subg, lambda_init)
    ssd_y_s, ssm_s, conv_s = ssd_sample(big_s, dt_s, state_ssm[0].reshape(bs, D_INNER, D_STATE), state_conv[0],
                                        convw, convb, dtb, alog, d_skip_x, norm_g, expand, gsum)
    x1_s = mix(attn_s.reshape(bs, 1024).astype(BF16), ssd_y_s, big_s, xs_, wa, wb, wm, g_post_mix[0])
    y_s = _post_mixer(x1_s, cache_mem_k[0].reshape(bs, n_mem, D_MODEL), cache_mem_v[0].reshape(bs, n_mem, D_MODEL),
                      w, bs, 1)

    return (
        y_p.reshape(bp, t, D_MODEL),
        y_s.reshape(bs, 1, D_MODEL),
        jnp.transpose(kt_p.reshape(1, bp, N_DIFF_KV, 2, HEAD_DIM, t), (0, 1, 5, 2, 3, 4)),
        v_p.reshape(1, bp, t, N_DIFF_KV, 2 * HEAD_DIM),
        k_s.reshape(1, bs, 1, N_DIFF_KV, 2, HEAD_DIM),
        v_s.reshape(1, bs, 1, N_DIFF_KV, 2 * HEAD_DIM),
        ssm_p.reshape(1, bp, N_SSM_HEADS, SSM_HEAD_DIM, D_STATE),
        conv_p.reshape(1, bp, CONV_WIDTH - 1, CONV_DIM),
        ssm_s.reshape(1, bs, N_SSM_HEADS, SSM_HEAD_DIM, D_STATE),
        conv_s.reshape(1, bs, CONV_WIDTH - 1, CONV_DIM),
        mk_p.reshape(1, bp, n_mem, N_MEM_HEADS, MEM_HEAD_DIM),
        mv_p.reshape(1, bp, n_mem, N_MEM_HEADS, MEM_HEAD_DIM),
    )
```
